```python
import jax, jax.numpy as jnp
from jax import lax
import numpy as np

D_MODEL = 4096
BATCH = 2
SEQ = 8192
DEPTH = 2

N_MIXERS = 2
N_MLA_LAYERS = (DEPTH + 1) // 2
N_SG_LAYERS = DEPTH // 2
MLA_HEADS = 32
QK_NOPE_DIM = 128
QK_ROPE_DIM = 64
V_HEAD_DIM = 128
Q_LORA_RANK = 1024
KV_LORA_RANK = 512
ROPE_THETA = 10000.0
Q_BLOCK = 128
SG_CHUNK = 128
SG_WIDTH = 3 * D_MODEL
SG_GROUPS = 32
SG_GROUP_DIM = SG_WIDTH // SG_GROUPS
FFN_DIM = 4 * D_MODEL
NORM_EPS = 1e-6
LN_EPS = 1e-5
N_MOD = 6

kernel_name = "adaln_hybrid_mla_spatial_gating_trunk"


def rms_norm(x, g):
    xf = x.astype(jnp.float32)
    y = xf * lax.rsqrt(jnp.mean(xf * xf, axis=-1, keepdims=True) + NORM_EPS)
    return (y * g.astype(jnp.float32)).astype(x.dtype)


def layer_norm(x, g, b):
    xf = x.astype(jnp.float32)
    mu = jnp.mean(xf, axis=-1, keepdims=True)
    var = jnp.mean(jnp.square(xf - mu), axis=-1, keepdims=True)
    y = (xf - mu) * lax.rsqrt(var + LN_EPS) * g.astype(jnp.float32) + b.astype(jnp.float32)
    return y.astype(x.dtype)


def modulate(h, shift, scale):
    return h * (1 + scale[:, None, :]) + shift[:, None, :]


def rope_tables(positions, dtype):
    inv_freq = jnp.power(ROPE_THETA, -jnp.arange(0, QK_ROPE_DIM, 2, dtype=jnp.float32) / QK_ROPE_DIM)
    ang = positions.astype(jnp.float32)[..., None] * inv_freq
    return jnp.cos(ang).astype(dtype), jnp.sin(ang).astype(dtype)


def apply_rope(x, cos, sin):
    x1, x2 = jnp.split(x, 2, axis=-1)
    return jnp.concatenate([x1 * cos - x2 * sin, x2 * cos + x1 * sin], axis=-1)


def mla_mixer(h, cos, sin, w_down, q_norm_g, w_uq, kv_norm_g, w_ukv, w_o):
    B, S, _ = h.shape
    down = h @ w_down
    c_q, c_kv, k_rope = jnp.split(down, [Q_LORA_RANK, Q_LORA_RANK + KV_LORA_RANK], axis=-1)
    q = (rms_norm(c_q, q_norm_g) @ w_uq).reshape(B, S, MLA_HEADS, QK_NOPE_DIM + QK_ROPE_DIM)
    q_nope, q_rope = jnp.split(q, [QK_NOPE_DIM], axis=-1)
    q_rope = apply_rope(q_rope, cos[:, :, None, :], sin[:, :, None, :])
    k_rope = apply_rope(k_rope, cos, sin)
    kv = (rms_norm(c_kv, kv_norm_g) @ w_ukv).reshape(B, S, MLA_HEADS, QK_NOPE_DIM + V_HEAD_DIM)
    k_nope, v = jnp.split(kv, [QK_NOPE_DIM], axis=-1)
    scale = (QK_NOPE_DIM + QK_ROPE_DIM) ** -0.5
    outs = []
    for blk in range(S // Q_BLOCK):
        q0, q1 = blk * Q_BLOCK, (blk + 1) * Q_BLOCK
        s = (jnp.einsum('bqhd,bkhd->bhqk', q_nope[:, q0:q1], k_nope[:, :q1])
             + jnp.einsum('bqhr,bkr->bhqk', q_rope[:, q0:q1], k_rope[:, :q1])).astype(jnp.float32) * scale
        causal = jnp.arange(q1)[None, :] <= (q0 + jnp.arange(Q_BLOCK))[:, None]
        s = jnp.where(causal, s, -jnp.inf)
        p = jax.nn.softmax(s, axis=-1).astype(v.dtype)
        outs.append(jnp.einsum('bhqk,bkhd->bqhd', p, v[:, :q1]))
    o = jnp.concatenate(outs, axis=1).reshape(B, S, MLA_HEADS * V_HEAD_DIM)
    return o @ w_o


def spatial_gating_mixer(h, w_in, v_norm_g, v_norm_b, w_s, b_s, w_out):
    B, S, _ = h.shape
    z = jax.nn.gelu(h @ w_in, approximate=False)
    u, v = jnp.split(z, 2, axis=-1)
    v = layer_norm(v, v_norm_g, v_norm_b)
    v = v.reshape(B, S // SG_CHUNK, SG_CHUNK, SG_GROUPS, SG_GROUP_DIM)
    causal = jnp.tril(jnp.ones((SG_CHUNK, SG_CHUNK), dtype=bool))
    w_causal = jnp.where(causal[None], w_s, 0)
    mixed = jnp.einsum('gts,bnsgc->bntgc', w_causal, v) + b_s.T[:, :, None]
    gated = u * mixed.reshape(B, S, SG_WIDTH)
    return gated @ w_out


def setup_inputs(seed: int = 0) -> dict:
    key = jax.random.key(seed)
    ks = jax.random.split(key, 24)
    f32 = jnp.float32

    def nrm(k, shape, scale):
        return jax.random.normal(k, shape, f32) * scale

    x = nrm(ks[0], (BATCH, SEQ, D_MODEL), 1.0)
    c = nrm(ks[1], (BATCH, D_MODEL), 1.0)
    offsets = jax.random.randint(ks[2], (BATCH,), 0, 4096, dtype=jnp.int32)
    positions = offsets[:, None] + jnp.arange(SEQ, dtype=jnp.int32)[None, :]
    L, LA, LS = DEPTH, N_MLA_LAYERS, N_SG_LAYERS
    down_dim = Q_LORA_RANK + KV_LORA_RANK + QK_ROPE_DIM
    return {
        "x": x,
        "c": c,
        "positions": positions,
        "ada_w": nrm(ks[3], (L, D_MODEL, N_MOD * D_MODEL), 0.5 * D_MODEL ** -0.5),
        "ada_b": nrm(ks[4], (L, N_MOD * D_MODEL), 0.01),
        "norm1_g": 1.0 + nrm(ks[5], (L, D_MODEL), 0.02),
        "norm2_g": 1.0 + nrm(ks[6], (L, D_MODEL), 0.02),
        "mla_w_down": nrm(ks[7], (LA, D_MODEL, down_dim), D_MODEL ** -0.5),
        "mla_q_norm_g": 1.0 + nrm(ks[8], (LA, Q_LORA_RANK), 0.02),
        "mla_w_uq": nrm(ks[9], (LA, Q_LORA_RANK, MLA_HEADS * (QK_NOPE_DIM + QK_ROPE_DIM)), Q_LORA_RANK ** -0.5),
        "mla_kv_norm_g": 1.0 + nrm(ks[10], (LA, KV_LORA_RANK), 0.02),
        "mla_w_ukv": nrm(ks[11], (LA, KV_LORA_RANK, MLA_HEADS * (QK_NOPE_DIM + V_HEAD_DIM)), KV_LORA_RANK ** -0.5),
        "mla_w_o": nrm(ks[12], (LA, MLA_HEADS * V_HEAD_DIM, D_MODEL), (MLA_HEADS * V_HEAD_DIM) ** -0.5),
        "sg_w_in": nrm(ks[13], (LS, D_MODEL, 2 * SG_WIDTH), D_MODEL ** -0.5),
        "sg_v_norm_g": 1.0 + nrm(ks[14], (LS, SG_WIDTH), 0.02),
        "sg_v_norm_b": nrm(ks[15], (LS, SG_WIDTH), 0.02),
        "sg_w_s": nrm(ks[16], (LS, SG_GROUPS, SG_CHUNK, SG_CHUNK), SG_CHUNK ** -0.5),
        "sg_b_s": 1.0 + nrm(ks[17], (LS, SG_GROUPS, SG_CHUNK), 0.02),
        "sg_w_out": nrm(ks[18], (LS, SG_WIDTH, D_MODEL), SG_WIDTH ** -0.5),
        "mlp_w1": nrm(ks[19], (L, D_MODEL, FFN_DIM), D_MODEL ** -0.5),
        "mlp_w2": nrm(ks[20], (L, FFN_DIM, D_MODEL), FFN_DIM ** -0.5),
        "final_norm_g": 1.0 + nrm(ks[21], (D_MODEL,), 0.02),
    }


def reference(x, c, positions, ada_w, ada_b, norm1_g, norm2_g, mla_w_down, mla_q_norm_g, mla_w_uq,
              mla_kv_norm_g, mla_w_ukv, mla_w_o, sg_w_in, sg_v_norm_g, sg_v_norm_b, sg_w_s, sg_b_s,
              sg_w_out, mlp_w1, mlp_w2, final_norm_g):
    cos, sin = rope_tables(positions, x.dtype)
    c_act = jax.nn.silu(c)
    h = x
    for i in range(DEPTH):
        mod = c_act @ ada_w[i] + ada_b[i]
        sh1, sc1, g1, sh2, sc2, g2 = jnp.split(mod, N_MOD, axis=-1)
        a = modulate(rms_norm(h, norm1_g[i]), sh1, sc1)
        j = i // N_MIXERS
        if i % N_MIXERS == 0:
            a = mla_mixer(a, cos, sin, mla_w_down[j], mla_q_norm_g[j], mla_w_uq[j],
                          mla_kv_norm_g[j], mla_w_ukv[j], mla_w_o[j])
        else:
            a = spatial_gating_mixer(a, sg_w_in[j], sg_v_norm_g[j], sg_v_norm_b[j],
                                     sg_w_s[j], sg_b_s[j], sg_w_out[j])
        h = h + g1[:, None, :] * a
        m = modulate(rms_norm(h, norm2_g[i]), sh2, sc2)
        m = jnp.square(jax.nn.relu(m @ mlp_w1[i])) @ mlp_w2[i]
        h = h + g2[:, None, :] * m
    return rms_norm(h, final_norm_g)
```

```python
import functools
import math

import jax
import jax.numpy as jnp
import numpy as np
from jax import lax
from jax.experimental import pallas as pl
from jax.experimental.pallas import tpu as pltpu

F32 = jnp.float32
BF16 = jnp.bfloat16

MLA_HEADS = 32
QK_NOPE_DIM = 128
QK_ROPE_DIM = 64
V_HEAD_DIM = 128
Q_LORA_RANK = 1024
KV_LORA_RANK = 512
ROPE_THETA = 10000.0
SG_CHUNK = 128
SG_GROUPS = 32
NORM_EPS = 1e-6
LN_EPS = 1e-5
N_MOD = 6

LANES = 128
QK_PAD_DIM = 256
HEAD_GROUP = 8
VMEM_LIMIT = 56 * 1024 * 1024
ROW_CHUNK = 128
COL_CHUNK = 1024


def _params(semantics):
    return pltpu.CompilerParams(dimension_semantics=semantics, vmem_limit_bytes=VMEM_LIMIT)


def _rms_mod(h, g, sh, sc):
    ms = jnp.mean(h * h, axis=-1, keepdims=True)
    y = h * lax.rsqrt(ms + NORM_EPS) * g
    return y * (1.0 + sc) + sh


def _rms(x, g):
    ms = jnp.mean(x * x, axis=-1, keepdims=True)
    return x * lax.rsqrt(ms + NORM_EPS) * g


def _ada_kernel(c_ref, w_ref, b_ref, o_ref, *, batch, tn):
    w = w_ref[0]
    for b in range(batch):
        c = c_ref[b]
        ca = c / (1.0 + jnp.exp(-c))
        cols = [jnp.sum(w[:, j * LANES:(j + 1) * LANES] * ca, axis=0, keepdims=True)
                for j in range(tn // LANES)]
        o_ref[0, b:b + 1, :] = jnp.concatenate(cols, axis=1) + b_ref[0]


def _ada_mod(c, ada_w, ada_b, tn=512):
    L, D, N = ada_w.shape
    B = c.shape[0]
    c_rep = jnp.broadcast_to(c[:, :, None], (B, D, LANES))
    return pl.pallas_call(
        functools.partial(_ada_kernel, batch=B, tn=tn),
        grid=(L, N // tn),
        in_specs=[
            pl.BlockSpec((B, D, LANES), lambda l, n: (0, 0, 0)),
            pl.BlockSpec((1, D, tn), lambda l, n: (l, 0, n)),
            pl.BlockSpec((1, 1, tn), lambda l, n: (l, 0, n)),
        ],
        out_specs=pl.BlockSpec((1, B, tn), lambda l, n: (l, 0, n)),
        out_shape=jax.ShapeDtypeStruct((L, B, N), F32),
        compiler_params=_params(("arbitrary", "arbitrary")),
        name="ada_mod",
    )(c_rep, ada_w, ada_b.reshape(L, 1, N))


def _rope_kernel(pos_ref, invf_ref, sign_ref, cos_ref, sin_ref):
    ang = pos_ref[...].astype(F32) * invf_ref[...]
    cos_ref[...] = jnp.cos(ang)
    sin_ref[...] = jnp.sin(ang) * sign_ref[...]


def _rope_tables(positions, tm=1024):
    T = positions.size
    half = QK_ROPE_DIM // 2
    inv_freq = jnp.power(ROPE_THETA, -jnp.arange(0, QK_ROPE_DIM, 2, dtype=F32) / QK_ROPE_DIM)
    invf = jnp.tile(inv_freq, LANES // half).reshape(1, LANES)
    sign = jnp.tile(jnp.concatenate([-jnp.ones((half,), F32), jnp.ones((half,), F32)]),
                    LANES // QK_ROPE_DIM).reshape(1, LANES)
    row = pl.BlockSpec((tm, LANES), lambda i: (i, 0))
    const = pl.BlockSpec((1, LANES), lambda i: (0, 0))
    return pl.pallas_call(
        _rope_kernel,
        grid=(T // tm,),
        in_specs=[pl.BlockSpec((tm, 1), lambda i: (i, 0)), const, const],
        out_specs=[row, row],
        out_shape=[jax.ShapeDtypeStruct((T, LANES), F32)] * 2,
        compiler_params=_params(("arbitrary",)),
        name="rope_tables",
    )(positions.reshape(T, 1), invf, sign)


def _down_kernel(h_ref, g_ref, sh_ref, sc_ref, w_ref, qg_ref, kvg_ref, cos_ref, sin_ref,
                 cq_ref, ckv_ref, kr_ref):
    a = _rms_mod(h_ref[0], g_ref[...], sh_ref[0], sc_ref[0])
    down = jnp.dot(a.astype(BF16), w_ref[...], preferred_element_type=F32)
    q_end = Q_LORA_RANK
    kv_end = q_end + KV_LORA_RANK
    cq_ref[0] = _rms(down[:, :q_end], qg_ref[...]).astype(BF16)
    ckv_ref[0] = _rms(down[:, q_end:kv_end], kvg_ref[...]).astype(BF16)
    roped = (down[:, kv_end:kv_end + LANES] * cos_ref[0]
             + down[:, kv_end + LANES:kv_end + 2 * LANES] * sin_ref[0])
    lane = lax.broadcasted_iota(jnp.int32, roped.shape, 1)
    kr_ref[0] = jnp.where(lane < QK_ROPE_DIM, roped, 0.0).astype(BF16)


def _mla_down(h, g, sh, sc, w_down_r, q_g, kv_g, cos, sin, tm=256):
    B, S, D = h.shape
    N = w_down_r.shape[1]
    row = lambda n: pl.BlockSpec((1, tm, n), lambda b, i: (b, i, 0))
    vec = lambda n: pl.BlockSpec((1, n), lambda b, i: (0, 0))
    bvec = pl.BlockSpec((1, 1, D), lambda b, i: (b, 0, 0))
    return pl.pallas_call(
        _down_kernel,
        grid=(B, S // tm),
        in_specs=[row(D), vec(D), bvec, bvec,
                  pl.BlockSpec((D, N), lambda b, i: (0, 0), pipeline_mode=pl.Buffered(1)),
                  vec(Q_LORA_RANK), vec(KV_LORA_RANK), row(LANES), row(LANES)],
        out_specs=[row(Q_LORA_RANK), row(KV_LORA_RANK), row(LANES)],
        out_shape=[jax.ShapeDtypeStruct((B, S, Q_LORA_RANK), BF16),
                   jax.ShapeDtypeStruct((B, S, KV_LORA_RANK), BF16),
                   jax.ShapeDtypeStruct((B, S, LANES), BF16)],
        compiler_params=_params(("arbitrary", "arbitrary")),
        name="mla_down",
    )(h, g, sh, sc, w_down_r, q_g, kv_g, cos, sin)


def _q_kernel(cq_ref, w_ref, cos_ref, sin_ref, q_ref):
    G = HEAD_GROUP
    r = jnp.dot(cq_ref[0], w_ref[...], preferred_element_type=F32)
    cos = cos_ref[0]
    sin = sin_ref[0]
    lane = lax.broadcasted_iota(jnp.int32, cos.shape, 1)
    low = lane < QK_ROPE_DIM
    nope_w = G * QK_NOPE_DIM
    rope_w = G * QK_ROPE_DIM
    for p in range(G // 2):
        x = r[:, nope_w + p * LANES:nope_w + (p + 1) * LANES]
        xs = r[:, nope_w + rope_w + p * LANES:nope_w + rope_w + (p + 1) * LANES]
        roped = x * cos + xs * sin
        halves = (roped, pltpu.roll(roped, QK_ROPE_DIM, 1))
        for k in range(2):
            j = 2 * p + k
            q_ref[0, j, :, :QK_NOPE_DIM] = r[:, j * QK_NOPE_DIM:(j + 1) * QK_NOPE_DIM].astype(BF16)
            q_ref[0, j, :, QK_NOPE_DIM:] = jnp.where(low, halves[k], 0.0).astype(BF16)


def _mla_q(cq, w_uq_r, cos, sin, tm=512):
    B, S, R = cq.shape
    G = HEAD_GROUP
    n_groups = MLA_HEADS // G
    wn = w_uq_r.shape[1] // n_groups
    return pl.pallas_call(
        _q_kernel,
        grid=(B, S // tm, n_groups),
        in_specs=[pl.BlockSpec((1, tm, R), lambda b, i, g: (b, i, 0)),
                  pl.BlockSpec((R, wn), lambda b, i, g: (0, g)),
                  pl.BlockSpec((1, tm, LANES), lambda b, i, g: (b, i, 0)),
                  pl.BlockSpec((1, tm, LANES), lambda b, i, g: (b, i, 0))],
        out_specs=pl.BlockSpec((1, G, tm, QK_PAD_DIM), lambda b, i, g: (b, g, i, 0)),
        out_shape=jax.ShapeDtypeStruct((B, MLA_HEADS, S, QK_PAD_DIM), BF16),
        compiler_params=_params(("arbitrary", "arbitrary", "arbitrary")),
        name="mla_q_up",
    )(cq, w_uq_r, cos, sin)


def _kv_kernel(ckv_ref, w_ref, kr_ref, k_ref, v_ref):
    r = jnp.dot(ckv_ref[0], w_ref[...], preferred_element_type=F32)
    kr = kr_ref[0]
    per_head = QK_NOPE_DIM + V_HEAD_DIM
    for j in range(HEAD_GROUP):
        k_ref[0, j, :, :QK_NOPE_DIM] = r[:, j * per_head:j * per_head + QK_NOPE_DIM].astype(BF16)
        k_ref[0, j, :, QK_NOPE_DIM:] = kr
        v_ref[0, j] = r[:, j * per_head + QK_NOPE_DIM:(j + 1) * per_head].astype(BF16)


def _mla_kv(ckv, w_ukv, kr, tm=512):
    B, S, R = ckv.shape
    G = HEAD_GROUP
    wn = G * (QK_NOPE_DIM + V_HEAD_DIM)
    return pl.pallas_call(
        _kv_kernel,
        grid=(B, S // tm, MLA_HEADS // G),
        in_specs=[pl.BlockSpec((1, tm, R), lambda b, i, g: (b, i, 0)),
                  pl.BlockSpec((R, wn), lambda b, i, g: (0, g)),
                  pl.BlockSpec((1, tm, LANES), lambda b, i, g: (b, i, 0))],
        out_specs=[pl.BlockSpec((1, G, tm, QK_PAD_DIM), lambda b, i, g: (b, g, i, 0)),
                   pl.BlockSpec((1, G, tm, V_HEAD_DIM), lambda b, i, g: (b, g, i, 0))],
        out_shape=[jax.ShapeDtypeStruct((B, MLA_HEADS, S, QK_PAD_DIM), BF16),
                   jax.ShapeDtypeStruct((B, MLA_HEADS, S, V_HEAD_DIM), BF16)],
        compiler_params=_params(("arbitrary", "arbitrary", "arbitrary")),
        name="mla_kv_up",
    )(ckv, w_ukv, kr)


def _attn_kernel(q_ref, k_ref, v_ref, o_ref, m_ref, l_ref, acc_ref, *, tq, scale):
    qi = pl.program_id(2)
    q = q_ref[0, 0]
    m_ref[...] = jnp.full(m_ref.shape, -jnp.inf, F32)
    l_ref[...] = jnp.zeros(l_ref.shape, F32)
    acc_ref[...] = jnp.zeros(acc_ref.shape, F32)

    def step(j, masked):
        start = pl.multiple_of(j * tq, tq)
        k = k_ref[0, 0, pl.ds(start, tq), :]
        v = v_ref[0, 0, pl.ds(start, tq), :]
        s = lax.dot_general(q, k, (((1,), (1,)), ((), ())), preferred_element_type=F32) * scale
        if masked:
            row = lax.broadcasted_iota(jnp.int32, s.shape, 0)
            col = lax.broadcasted_iota(jnp.int32, s.shape, 1)
            s = jnp.where(col <= row, s, -jnp.inf)
        m_prev = m_ref[...]
        m_new = jnp.maximum(m_prev, jnp.max(s, axis=-1, keepdims=True))
        alpha = jnp.exp(m_prev - m_new)
        p = jnp.exp(s - m_new)
        l_ref[...] = alpha * l_ref[...] + jnp.sum(p, axis=-1, keepdims=True)
        acc_ref[...] = alpha * acc_ref[...] + jnp.dot(p.astype(BF16), v, preferred_element_type=F32)
        m_ref[...] = m_new

    def body(j, carry):
        step(j, False)
        return carry

    lax.fori_loop(0, qi, body, 0)
    step(qi, True)
    o_ref[0] = (acc_ref[...] / l_ref[...]).astype(o_ref.dtype)


def _mla_attention(q, k, v, tq=512):
    B, H, S, _ = q.shape
    scale = (QK_NOPE_DIM + QK_ROPE_DIM) ** -0.5
    return pl.pallas_call(
        functools.partial(_attn_kernel, tq=tq, scale=scale),
        grid=(B, H, S // tq),
        in_specs=[pl.BlockSpec((1, 1, tq, QK_PAD_DIM), lambda b, h, i: (b, h, i, 0)),
                  pl.BlockSpec((1, 1, S, QK_PAD_DIM), lambda b, h, i: (b, h, 0, 0)),
                  pl.BlockSpec((1, 1, S, V_HEAD_DIM), lambda b, h, i: (b, h, 0, 0))],
        out_specs=pl.BlockSpec((1, tq, V_HEAD_DIM), lambda b, h, i: (b, i, h)),
        out_shape=jax.ShapeDtypeStruct((B, S, H * V_HEAD_DIM), BF16),
        scratch_shapes=[pltpu.VMEM((tq, 1), F32), pltpu.VMEM((tq, 1), F32),
                        pltpu.VMEM((tq, V_HEAD_DIM), F32)],
        compiler_params=_params(("arbitrary", "arbitrary", "arbitrary")),
        name="mla_attention",
    )(q, k, v)


def _proj_res_kernel(a_ref, w_ref, h_ref, g_ref, o_ref):
    r = jnp.dot(a_ref[0], w_ref[...], preferred_element_type=F32)
    o_ref[0] = h_ref[0] + g_ref[0] * r


def _proj_residual(a, w, h, gate, tm=512, tn=1024):
    B, S, K = a.shape
    N = w.shape[1]
    return pl.pallas_call(
        _proj_res_kernel,
        grid=(B, S // tm, N // tn),
        in_specs=[pl.BlockSpec((1, tm, K), lambda b, i, n: (b, i, 0)),
                  pl.BlockSpec((K, tn), lambda b, i, n: (0, n)),
                  pl.BlockSpec((1, tm, tn), lambda b, i, n: (b, i, n)),
                  pl.BlockSpec((1, 1, tn), lambda b, i, n: (b, 0, n))],
        out_specs=pl.BlockSpec((1, tm, tn), lambda b, i, n: (b, i, n)),
        out_shape=jax.ShapeDtypeStruct((B, S, N), F32),
        compiler_params=_params(("arbitrary", "arbitrary", "arbitrary")),
        name="proj_residual",
    )(a, w, h, gate)


def _mlp_kernel(h_ref, g_ref, sh_ref, sc_ref, gate_ref, w1_ref, w2_ref, fg_ref, o_ref, a_ref,
                *, final_norm):
    f = pl.program_id(2)
    tm, D = a_ref.shape

    @pl.when(f == 0)
    def _():
        for r in range(0, tm, ROW_CHUNK):
            rows = slice(r, r + ROW_CHUNK)
            a_ref[rows] = _rms_mod(h_ref[0, rows], g_ref[...], sh_ref[0], sc_ref[0]).astype(BF16)
        o_ref[0] = jnp.zeros((tm, D), F32)

    hid = jnp.dot(a_ref[...], w1_ref[...], preferred_element_type=F32)
    hid = jnp.square(jnp.maximum(hid, 0.0)).astype(BF16)
    for n in range(0, D, COL_CHUNK):
        cols = slice(n, n + COL_CHUNK)
        o_ref[0, :, cols] += jnp.dot(hid, w2_ref[:, cols], preferred_element_type=F32)

    @pl.when(f == pl.num_programs(2) - 1)
    def _():
        for r in range(0, tm, ROW_CHUNK):
            rows = slice(r, r + ROW_CHUNK)
            out = h_ref[0, rows] + gate_ref[0] * o_ref[0, rows]
            if final_norm:
                out = _rms(out, fg_ref[...])
            o_ref[0, rows] = out


def _mlp(h, g, sh, sc, gate, w1, w2, final_g, final_norm, tm=512, tf=512):
    B, S, D = h.shape
    F = w1.shape[1]
    vec = pl.BlockSpec((1, D), lambda b, i, f: (0, 0))
    bvec = pl.BlockSpec((1, 1, D), lambda b, i, f: (b, 0, 0))
    return pl.pallas_call(
        functools.partial(_mlp_kernel, final_norm=final_norm),
        grid=(B, S // tm, F // tf),
        in_specs=[pl.BlockSpec((1, tm, D), lambda b, i, f: (b, i, 0), pipeline_mode=pl.Buffered(1)),
                  vec, bvec, bvec, bvec,
                  pl.BlockSpec((D, tf), lambda b, i, f: (0, f)),
                  pl.BlockSpec((tf, D), lambda b, i, f: (f, 0)),
                  vec],
        out_specs=pl.BlockSpec((1, tm, D), lambda b, i, f: (b, i, 0)),
        out_shape=jax.ShapeDtypeStruct((B, S, D), F32),
        scratch_shapes=[pltpu.VMEM((tm, D), BF16)],
        compiler_params=_params(("arbitrary", "arbitrary", "arbitrary")),
        name="mlp",
    )(h, g, sh, sc, gate, w1, w2, final_g)


def _sg_in_kernel(h_ref, g_ref, sh_ref, sc_ref, w_ref, z_ref, mu_ref, rstd_ref, a_ref, s1_ref, s2_ref,
                  *, n_half, width):
    n = pl.program_id(2)

    @pl.when(n == 0)
    def _():
        for r in range(0, a_ref.shape[0], ROW_CHUNK):
            rows = slice(r, r + ROW_CHUNK)
            a_ref[rows] = _rms_mod(h_ref[0, rows], g_ref[...], sh_ref[0], sc_ref[0]).astype(BF16)
        s1_ref[...] = jnp.zeros(s1_ref.shape, F32)
        s2_ref[...] = jnp.zeros(s2_ref.shape, F32)

    x = jnp.dot(a_ref[...], w_ref[...], preferred_element_type=F32)
    z = 0.5 * x * (1.0 + lax.erf(x * math.sqrt(0.5)))
    z_ref[0] = z

    @pl.when(n >= n_half)
    def _():
        s1_ref[...] += jnp.sum(z, axis=-1, keepdims=True)
        s2_ref[...] += jnp.sum(z * z, axis=-1, keepdims=True)

    @pl.when(n == pl.num_programs(2) - 1)
    def _():
        mu = s1_ref[...] / width
        var = s2_ref[...] / width - mu * mu
        mu_ref[0] = jnp.broadcast_to(mu, mu_ref.shape[1:])
        rstd_ref[0] = jnp.broadcast_to(lax.rsqrt(var + LN_EPS), rstd_ref.shape[1:])


def _sg_in(h, g, sh, sc, w_in, tm=512, tn=512):
    B, S, D = h.shape
    N = w_in.shape[1]
    width = N // 2
    vec = pl.BlockSpec((1, D), lambda b, i, n: (0, 0))
    bvec = pl.BlockSpec((1, 1, D), lambda b, i, n: (b, 0, 0))
    stat = pl.BlockSpec((1, tm, LANES), lambda b, i, n: (b, i, 0))
    return pl.pallas_call(
        functools.partial(_sg_in_kernel, n_half=width // tn, width=float(width)),
        grid=(B, S // tm, N // tn),
        in_specs=[pl.BlockSpec((1, tm, D), lambda b, i, n: (b, i, 0), pipeline_mode=pl.Buffered(1)),
                  vec, bvec, bvec,
                  pl.BlockSpec((D, tn), lambda b, i, n: (0, n))],
        out_specs=[pl.BlockSpec((1, tm, tn), lambda b, i, n: (b, i, n)), stat, stat],
        out_shape=[jax.ShapeDtypeStruct((B, S, N), F32),
                   jax.ShapeDtypeStruct((B, S, LANES), F32),
                   jax.ShapeDtypeStruct((B, S, LANES), F32)],
        scratch_shapes=[pltpu.VMEM((tm, D), BF16), pltpu.VMEM((tm, 1), F32), pltpu.VMEM((tm, 1), F32)],
        compiler_params=_params(("arbitrary", "arbitrary", "arbitrary")),
        name="sg_in",
    )(h, g, sh, sc, w_in)


def _sg_out_kernel(u_ref, v_ref, mu_ref, rstd_ref, lng_ref, lnb_ref, ws_ref, bs_ref, w_ref,
                   h_ref, gate_ref, o_ref, gated_ref, *, groups_per_step, group_dim):
    kk = pl.program_id(2)
    tm = u_ref.shape[1]
    T = SG_CHUNK
    row = lax.broadcasted_iota(jnp.int32, (T, T), 0)
    col = lax.broadcasted_iota(jnp.int32, (T, T), 1)
    reps = group_dim // LANES
    for gg in range(groups_per_step):
        ws = jnp.where(col <= row, ws_ref[gg], 0.0).astype(BF16)
        bias = jnp.concatenate([bs_ref[gg]] * reps, axis=1)
        c0, c1 = gg * group_dim, (gg + 1) * group_dim
        lng = lng_ref[:, c0:c1]
        lnb = lnb_ref[:, c0:c1]
        for c in range(tm // T):
            r0, r1 = c * T, (c + 1) * T
            mu = mu_ref[0, r0:r1, 0:1]
            rstd = rstd_ref[0, r0:r1, 0:1]
            vln = (v_ref[0, r0:r1, c0:c1] - mu) * rstd * lng + lnb
            mixed = jnp.dot(ws, vln.astype(BF16), preferred_element_type=F32) + bias
            gated_ref[r0:r1, c0:c1] = (u_ref[0, r0:r1, c0:c1] * mixed).astype(BF16)

    @pl.when(kk == 0)
    def _():
        o_ref[0] = jnp.zeros(o_ref.shape[1:], F32)

    for n in range(0, o_ref.shape[2], COL_CHUNK):
        cols = slice(n, n + COL_CHUNK)
        o_ref[0, :, cols] += jnp.dot(gated_ref[...], w_ref[:, cols], preferred_element_type=F32)

    @pl.when(kk == pl.num_programs(2) - 1)
    def _():
        for r in range(0, tm, ROW_CHUNK):
            rows = slice(r, r + ROW_CHUNK)
            o_ref[0, rows] = h_ref[0, rows] + gate_ref[0] * o_ref[0, rows]


def _sg_out(z, mu, rstd, ln_g, ln_b, w_s, bs_rep, w_out, h, gate, tm=512, groups_per_step=2):
    B, S, D = h.shape
    W = w_out.shape[0]
    group_dim = W // SG_GROUPS
    kc = groups_per_step * group_dim
    n_k = W // kc
    stat = pl.BlockSpec((1, tm, LANES), lambda b, i, k: (b, i, 0))
    return pl.pallas_call(
        functools.partial(_sg_out_kernel, groups_per_step=groups_per_step, group_dim=group_dim),
        grid=(B, S // tm, n_k),
        in_specs=[pl.BlockSpec((1, tm, kc), lambda b, i, k: (b, i, k)),
                  pl.BlockSpec((1, tm, kc), lambda b, i, k: (b, i, k + n_k)),
                  stat, stat,
                  pl.BlockSpec((1, kc), lambda b, i, k: (0, k)),
                  pl.BlockSpec((1, kc), lambda b, i, k: (0, k)),
                  pl.BlockSpec((groups_per_step, SG_CHUNK, SG_CHUNK), lambda b, i, k: (k, 0, 0)),
                  pl.BlockSpec((groups_per_step, SG_CHUNK, LANES), lambda b, i, k: (k, 0, 0)),
                  pl.BlockSpec((kc, D), lambda b, i, k: (k, 0)),
                  pl.BlockSpec((1, tm, D), lambda b, i, k: (b, i, 0), pipeline_mode=pl.Buffered(1)),
                  pl.BlockSpec((1, 1, D), lambda b, i, k: (b, 0, 0))],
        out_specs=pl.BlockSpec((1, tm, D), lambda b, i, k: (b, i, 0)),
        out_shape=jax.ShapeDtypeStruct((B, S, D), F32),
        scratch_shapes=[pltpu.VMEM((tm, kc), BF16)],
        compiler_params=_params(("arbitrary", "arbitrary", "arbitrary")),
        name="sg_out",
    )(z, z, mu, rstd, ln_g, ln_b, w_s, bs_rep, w_out, h, gate)


def _down_weight_layout(w_down):
    lat = Q_LORA_RANK + KV_LORA_RANK
    half = QK_ROPE_DIM // 2
    rope = w_down[:, lat:lat + QK_ROPE_DIM]
    swapped = jnp.concatenate([rope[:, half:], rope[:, :half]], axis=1)
    return jnp.concatenate([w_down[:, :lat], rope, rope, swapped, swapped], axis=1)


def _uq_weight_layout(w_uq):
    R = w_uq.shape[0]
    G = HEAD_GROUP
    half = QK_ROPE_DIM // 2
    w = w_uq.reshape(R, MLA_HEADS // G, G, QK_NOPE_DIM + QK_ROPE_DIM)
    nope = w[..., :QK_NOPE_DIM].reshape(R, MLA_HEADS // G, G * QK_NOPE_DIM)
    rope = w[..., QK_NOPE_DIM:]
    swapped = jnp.concatenate([rope[..., half:], rope[..., :half]], axis=-1)
    rope = rope.reshape(R, MLA_HEADS // G, G * QK_ROPE_DIM)
    swapped = swapped.reshape(R, MLA_HEADS // G, G * QK_ROPE_DIM)
    return jnp.concatenate([nope, rope, swapped], axis=-1).reshape(R, -1)


def kernel(x, c, positions, ada_w, ada_b, norm1_g, norm2_g, mla_w_down, mla_q_norm_g, mla_w_uq,
           mla_kv_norm_g, mla_w_ukv, mla_w_o, sg_w_in, sg_v_norm_g, sg_v_norm_b, sg_w_s, sg_b_s,
           sg_w_out, mlp_w1, mlp_w2, final_norm_g):
    B, S, D = x.shape
    depth = ada_w.shape[0]
    cos, sin = _rope_tables(positions)
    cos = cos.reshape(B, S, LANES)
    sin = sin.reshape(B, S, LANES)
    mod = _ada_mod(c, ada_w, ada_b).reshape(depth, B, N_MOD, 1, D)
    final_g = final_norm_g.reshape(1, D)

    h = x
    for i in range(depth):
        sh1, sc1, g1, sh2, sc2, g2 = [mod[i, :, m] for m in range(N_MOD)]
        n1 = norm1_g[i].reshape(1, D)
        j = i // 2
        if i % 2 == 0:
            w_down_r = _down_weight_layout(mla_w_down[j]).astype(BF16)
            w_uq_r = _uq_weight_layout(mla_w_uq[j]).astype(BF16)
            cq, ckv, kr = _mla_down(h, n1, sh1, sc1, w_down_r,
                                    mla_q_norm_g[j].reshape(1, -1), mla_kv_norm_g[j].reshape(1, -1),
                                    cos, sin)
            q = _mla_q(cq, w_uq_r, cos, sin)
            k, v = _mla_kv(ckv, mla_w_ukv[j].astype(BF16), kr)
            o = _mla_attention(q, k, v)
            h = _proj_residual(o, mla_w_o[j].astype(BF16), h, g1)
        else:
            z, mu, rstd = _sg_in(h, n1, sh1, sc1, sg_w_in[j].astype(BF16))
            bs_rep = jnp.broadcast_to(sg_b_s[j][:, :, None], (SG_GROUPS, SG_CHUNK, LANES))
            h = _sg_out(z, mu, rstd, sg_v_norm_g[j].reshape(1, -1), sg_v_norm_b[j].reshape(1, -1),
                        sg_w_s[j], bs_rep, sg_w_out[j].astype(BF16), h, g1)
        h = _mlp(h, norm2_g[i].reshape(1, D), sh2, sc2, g2,
                 mlp_w1[i].astype(BF16), mlp_w2[i].astype(BF16), final_g,
                 final_norm=(i == depth - 1))
    return h
```

```python
import functools
import math

import jax
import jax.numpy as jnp
import numpy as np
from jax import lax
from jax.experimental import pallas as pl
from jax.experimental.pallas import tpu as pltpu

F32 = jnp.float32
BF16 = jnp.bfloat16

MLA_HEADS = 32
QK_NOPE_DIM = 128
QK_ROPE_DIM = 64
V_HEAD_DIM = 128
Q_LORA_RANK = 1024
KV_LORA_RANK = 512
ROPE_THETA = 10000.0
SG_CHUNK = 128
SG_GROUPS = 32
NORM_EPS = 1e-6
LN_EPS = 1e-5
N_MOD = 6

LANES = 128
QK_PAD_DIM = 256
HEAD_GROUP = 8
VMEM_LIMIT = 56 * 1024 * 1024
ROW_CHUNK = 128
COL_CHUNK = 1024
DOT_ROWS = 256


def _params(semantics):
    return pltpu.CompilerParams(dimension_semantics=semantics, vmem_limit_bytes=VMEM_LIMIT)


def _rms_mod(h, g, sh, sc):
    ms = jnp.mean(h * h, axis=-1, keepdims=True)
    y = h * lax.rsqrt(ms + NORM_EPS) * g
    return y * (1.0 + sc) + sh


def _rms(x, g):
    ms = jnp.mean(x * x, axis=-1, keepdims=True)
    return x * lax.rsqrt(ms + NORM_EPS) * g


def _ada_kernel(c_ref, w_ref, b_ref, o_ref, *, batch, tn):
    w = w_ref[0]
    for b in range(batch):
        c = c_ref[b]
        ca = c / (1.0 + jnp.exp(-c))
        cols = [jnp.sum(w[:, j * LANES:(j + 1) * LANES] * ca, axis=0, keepdims=True)
                for j in range(tn // LANES)]
        o_ref[0, b:b + 1, :] = jnp.concatenate(cols, axis=1) + b_ref[0]


def _ada_mod(c, ada_w, ada_b, tn=512):
    L, D, N = ada_w.shape
    B = c.shape[0]
    c_rep = jnp.broadcast_to(c[:, :, None], (B, D, LANES))
    return pl.pallas_call(
        functools.partial(_ada_kernel, batch=B, tn=tn),
        grid=(L, N // tn),
        in_specs=[
            pl.BlockSpec((B, D, LANES), lambda l, n: (0, 0, 0)),
            pl.BlockSpec((1, D, tn), lambda l, n: (l, 0, n)),
            pl.BlockSpec((1, 1, tn), lambda l, n: (l, 0, n)),
        ],
        out_specs=pl.BlockSpec((1, B, tn), lambda l, n: (l, 0, n)),
        out_shape=jax.ShapeDtypeStruct((L, B, N), F32),
        compiler_params=_params(("arbitrary", "arbitrary")),
        name="ada_mod",
    )(c_rep, ada_w, ada_b.reshape(L, 1, N))


def _rope_kernel(pos_ref, invf_ref, sign_ref, cos_ref, sin_ref):
    ang = pos_ref[...].astype(F32) * invf_ref[...]
    cos_ref[...] = jnp.cos(ang)
    sin_ref[...] = jnp.sin(ang) * sign_ref[...]


def _rope_tables(positions, tm=1024):
    T = positions.size
    half = QK_ROPE_DIM // 2
    inv_freq = jnp.power(ROPE_THETA, -jnp.arange(0, QK_ROPE_DIM, 2, dtype=F32) / QK_ROPE_DIM)
    invf = jnp.tile(inv_freq, LANES // half).reshape(1, LANES)
    sign = jnp.tile(jnp.concatenate([-jnp.ones((half,), F32), jnp.ones((half,), F32)]),
                    LANES // QK_ROPE_DIM).reshape(1, LANES)
    row = pl.BlockSpec((tm, LANES), lambda i: (i, 0))
    const = pl.BlockSpec((1, LANES), lambda i: (0, 0))
    return pl.pallas_call(
        _rope_kernel,
        grid=(T // tm,),
        in_specs=[pl.BlockSpec((tm, 1), lambda i: (i, 0)), const, const],
        out_specs=[row, row],
        out_shape=[jax.ShapeDtypeStruct((T, LANES), F32)] * 2,
        compiler_params=_params(("arbitrary",)),
        name="rope_tables",
    )(positions.reshape(T, 1), invf, sign)


def _down_kernel(h_ref, g_ref, sh_ref, sc_ref, w_ref, qg_ref, kvg_ref, cos_ref, sin_ref,
                 cq_ref, ckv_ref, kr_ref):
    a = _rms_mod(h_ref[0], g_ref[...], sh_ref[0], sc_ref[0])
    down = jnp.dot(a.astype(BF16), w_ref[...], preferred_element_type=F32)
    q_end = Q_LORA_RANK
    kv_end = q_end + KV_LORA_RANK
    cq_ref[0] = _rms(down[:, :q_end], qg_ref[...]).astype(BF16)
    ckv_ref[0] = _rms(down[:, q_end:kv_end], kvg_ref[...]).astype(BF16)
    roped = (down[:, kv_end:kv_end + LANES] * cos_ref[0]
             + down[:, kv_end + LANES:kv_end + 2 * LANES] * sin_ref[0])
    lane = lax.broadcasted_iota(jnp.int32, roped.shape, 1)
    kr_ref[0] = jnp.where(lane < QK_ROPE_DIM, roped, 0.0).astype(BF16)


def _mla_down(h, g, sh, sc, w_down_r, q_g, kv_g, cos, sin, tm=256):
    B, S, D = h.shape
    N = w_down_r.shape[1]
    row = lambda n: pl.BlockSpec((1, tm, n), lambda b, i: (b, i, 0))
    vec = lambda n: pl.BlockSpec((1, n), lambda b, i: (0, 0))
    bvec = pl.BlockSpec((1, 1, D), lambda b, i: (b, 0, 0))
    return pl.pallas_call(
        _down_kernel,
        grid=(B, S // tm),
        in_specs=[row(D), vec(D), bvec, bvec,
                  pl.BlockSpec((D, N), lambda b, i: (0, 0), pipeline_mode=pl.Buffered(1)),
                  vec(Q_LORA_RANK), vec(KV_LORA_RANK), row(LANES), row(LANES)],
        out_specs=[row(Q_LORA_RANK), row(KV_LORA_RANK), row(LANES)],
        out_shape=[jax.ShapeDtypeStruct((B, S, Q_LORA_RANK), BF16),
                   jax.ShapeDtypeStruct((B, S, KV_LORA_RANK), BF16),
                   jax.ShapeDtypeStruct((B, S, LANES), BF16)],
        compiler_params=_params(("arbitrary", "arbitrary")),
        name="mla_down",
    )(h, g, sh, sc, w_down_r, q_g, kv_g, cos, sin)


def _q_kernel(cq_ref, w_ref, cos_ref, sin_ref, q_ref):
    G = HEAD_GROUP
    r = jnp.dot(cq_ref[0], w_ref[...], preferred_element_type=F32)
    cos = cos_ref[0]
    sin = sin_ref[0]
    lane = lax.broadcasted_iota(jnp.int32, cos.shape, 1)
    low = lane < QK_ROPE_DIM
    nope_w = G * QK_NOPE_DIM
    rope_w = G * QK_ROPE_DIM
    for p in range(G // 2):
        x = r[:, nope_w + p * LANES:nope_w + (p + 1) * LANES]
        xs = r[:, nope_w + rope_w + p * LANES:nope_w + rope_w + (p + 1) * LANES]
        roped = x * cos + xs * sin
        halves = (roped, pltpu.roll(roped, QK_ROPE_DIM, 1))
        for k in range(2):
            j = 2 * p + k
            q_ref[0, j, :, :QK_NOPE_DIM] = r[:, j * QK_NOPE_DIM:(j + 1) * QK_NOPE_DIM].astype(BF16)
            q_ref[0, j, :, QK_NOPE_DIM:] = jnp.where(low, halves[k], 0.0).astype(BF16)


def _mla_q(cq, w_uq_r, cos, sin, tm=512):
    B, S, R = cq.shape
    G = HEAD_GROUP
    n_groups = MLA_HEADS // G
    wn = w_uq_r.shape[1] // n_groups
    return pl.pallas_call(
        _q_kernel,
        grid=(B, S // tm, n_groups),
        in_specs=[pl.BlockSpec((1, tm, R), lambda b, i, g: (b, i, 0)),
                  pl.BlockSpec((R, wn), lambda b, i, g: (0, g)),
                  pl.BlockSpec((1, tm, LANES), lambda b, i, g: (b, i, 0)),
                  pl.BlockSpec((1, tm, LANES), lambda b, i, g: (b, i, 0))],
        out_specs=pl.BlockSpec((1, G, tm, QK_PAD_DIM), lambda b, i, g: (b, g, i, 0)),
        out_shape=jax.ShapeDtypeStruct((B, MLA_HEADS, S, QK_PAD_DIM), BF16),
        compiler_params=_params(("arbitrary", "arbitrary", "arbitrary")),
        name="mla_q_up",
    )(cq, w_uq_r, cos, sin)


def _kv_kernel(ckv_ref, wk_ref, wvt_ref, kr_ref, k_ref, vt_ref):
    ckv = ckv_ref[0]
    kn = jnp.dot(ckv, wk_ref[...], preferred_element_type=F32)
    vt = lax.dot_general(wvt_ref[...], ckv, (((1,), (1,)), ((), ())),
                         preferred_element_type=F32)
    kr = kr_ref[0]
    for j in range(HEAD_GROUP):
        k_ref[0, j, :, :QK_NOPE_DIM] = kn[:, j * QK_NOPE_DIM:(j + 1) * QK_NOPE_DIM].astype(BF16)
        k_ref[0, j, :, QK_NOPE_DIM:] = kr
        vt_ref[0, j] = vt[j * V_HEAD_DIM:(j + 1) * V_HEAD_DIM, :].astype(BF16)


def _mla_kv(ckv, w_k, w_vt, kr, tm=512):
    B, S, R = ckv.shape
    G = HEAD_GROUP
    return pl.pallas_call(
        _kv_kernel,
        grid=(B, S // tm, MLA_HEADS // G),
        in_specs=[pl.BlockSpec((1, tm, R), lambda b, i, g: (b, i, 0)),
                  pl.BlockSpec((R, G * QK_NOPE_DIM), lambda b, i, g: (0, g)),
                  pl.BlockSpec((G * V_HEAD_DIM, R), lambda b, i, g: (g, 0)),
                  pl.BlockSpec((1, tm, LANES), lambda b, i, g: (b, i, 0))],
        out_specs=[pl.BlockSpec((1, G, tm, QK_PAD_DIM), lambda b, i, g: (b, g, i, 0)),
                   pl.BlockSpec((1, G, V_HEAD_DIM, tm), lambda b, i, g: (b, g, 0, i))],
        out_shape=[jax.ShapeDtypeStruct((B, MLA_HEADS, S, QK_PAD_DIM), BF16),
                   jax.ShapeDtypeStruct((B, MLA_HEADS, V_HEAD_DIM, S), BF16)],
        compiler_params=_params(("arbitrary", "arbitrary", "arbitrary")),
        name="mla_kv_up",
    )(ckv, w_k, w_vt, kr)


def _attn_kernel(q_ref, k_ref, vt_ref, o_ref, qt_ref, sa_ref, sb_ref, m_ref, l_ref, acc_ref,
                 *, tq, exp2_scale):
    qi = pl.program_id(2)
    qt_ref[...] = q_ref[0, 0].T
    m_ref[...] = jnp.full(m_ref.shape, -jnp.inf, F32)
    l_ref[...] = jnp.zeros(l_ref.shape, F32)
    acc_ref[...] = jnp.zeros(acc_ref.shape, F32)

    def scores(j, s_ref):
        start = pl.multiple_of(j * tq, tq)
        s_ref[...] = jnp.dot(k_ref[0, 0, pl.ds(start, tq), :], qt_ref[...],
                             preferred_element_type=F32)

    def update(j, s_ref, masked):
        start = pl.multiple_of(j * tq, tq)
        s = s_ref[...]
        if masked:
            key = lax.broadcasted_iota(jnp.int32, s.shape, 0)
            qry = lax.broadcasted_iota(jnp.int32, s.shape, 1)
            s = jnp.where(key <= qry, s, -jnp.inf)
        m_prev = m_ref[...]
        m_new = jnp.maximum(m_prev, jnp.max(s, axis=0, keepdims=True))
        alpha = jnp.exp2((m_prev - m_new) * exp2_scale)
        p = jnp.exp2((s - m_new) * exp2_scale)
        l_ref[...] = alpha * l_ref[...] + jnp.sum(p, axis=0, keepdims=True)
        pv = jnp.dot(vt_ref[0, 0, :, pl.ds(start, tq)], p.astype(BF16), preferred_element_type=F32)
        acc_ref[...] = alpha * acc_ref[...] + pv
        m_ref[...] = m_new

    scores(0, sa_ref)

    def pair(i, carry):
        j = 2 * i
        scores(j + 1, sb_ref)
        update(j, sa_ref, False)
        scores(j + 2, sa_ref)
        update(j + 1, sb_ref, False)
        return carry

    lax.fori_loop(0, qi // 2, pair, 0)

    @pl.when(qi % 2 == 1)
    def _():
        scores(qi, sb_ref)
        update(qi - 1, sa_ref, False)
        update(qi, sb_ref, True)

    @pl.when(qi % 2 == 0)
    def _():
        update(qi, sa_ref, True)

    o_ref[0] = (acc_ref[...] / l_ref[...]).T.astype(o_ref.dtype)


def _mla_attention(q, k, vt, tq=512):
    B, H, S, _ = q.shape
    scale = (QK_NOPE_DIM + QK_ROPE_DIM) ** -0.5
    return pl.pallas_call(
        functools.partial(_attn_kernel, tq=tq, exp2_scale=scale * math.log2(math.e)),
        grid=(B, H, S // tq),
        in_specs=[pl.BlockSpec((1, 1, tq, QK_PAD_DIM), lambda b, h, i: (b, h, i, 0)),
                  pl.BlockSpec((1, 1, S, QK_PAD_DIM), lambda b, h, i: (b, h, 0, 0)),
                  pl.BlockSpec((1, 1, V_HEAD_DIM, S), lambda b, h, i: (b, h, 0, 0))],
        out_specs=pl.BlockSpec((1, tq, V_HEAD_DIM), lambda b, h, i: (b, i, h)),
        out_shape=jax.ShapeDtypeStruct((B, S, H * V_HEAD_DIM), BF16),
        scratch_shapes=[pltpu.VMEM((QK_PAD_DIM, tq), BF16),
                        pltpu.VMEM((tq, tq), F32), pltpu.VMEM((tq, tq), F32),
                        pltpu.VMEM((1, tq), F32), pltpu.VMEM((1, tq), F32),
                        pltpu.VMEM((V_HEAD_DIM, tq), F32)],
        compiler_params=_params(("arbitrary", "arbitrary", "arbitrary")),
        name="mla_attention",
    )(q, k, vt)


def _proj_res_kernel(a_ref, w_ref, h_ref, g_ref, o_ref):
    r = jnp.dot(a_ref[0], w_ref[...], preferred_element_type=F32)
    o_ref[0] = h_ref[0] + g_ref[0] * r


def _proj_residual(a, w, h, gate, tm=512, tn=1024):
    B, S, K = a.shape
    N = w.shape[1]
    return pl.pallas_call(
        _proj_res_kernel,
        grid=(B, S // tm, N // tn),
        in_specs=[pl.BlockSpec((1, tm, K), lambda b, i, n: (b, i, 0)),
                  pl.BlockSpec((K, tn), lambda b, i, n: (0, n)),
                  pl.BlockSpec((1, tm, tn), lambda b, i, n: (b, i, n)),
                  pl.BlockSpec((1, 1, tn), lambda b, i, n: (b, 0, n))],
        out_specs=pl.BlockSpec((1, tm, tn), lambda b, i, n: (b, i, n)),
        out_shape=jax.ShapeDtypeStruct((B, S, N), F32),
        compiler_params=_params(("arbitrary", "arbitrary", "arbitrary")),
        name="proj_residual",
    )(a, w, h, gate)


def _mlp_kernel(h_ref, g_ref, sh_ref, sc_ref, gate_ref, w1_ref, w2_ref, fg_ref, o_ref, a_ref,
                *, final_norm):
    f = pl.program_id(2)
    tm, D = a_ref.shape

    @pl.when(f == 0)
    def _():
        for r in range(0, tm, ROW_CHUNK):
            rows = slice(r, r + ROW_CHUNK)
            a_ref[rows] = _rms_mod(h_ref[0, rows], g_ref[...], sh_ref[0], sc_ref[0]).astype(BF16)
        o_ref[0] = jnp.zeros((tm, D), F32)

    hid = jnp.dot(a_ref[...], w1_ref[...], preferred_element_type=F32)
    hid = jnp.square(jnp.maximum(hid, 0.0)).astype(BF16)
    for n in range(0, D, COL_CHUNK):
        cols = slice(n, n + COL_CHUNK)
        o_ref[0, :, cols] += jnp.dot(hid, w2_ref[:, cols], preferred_element_type=F32)

    @pl.when(f == pl.num_programs(2) - 1)
    def _():
        for r in range(0, tm, ROW_CHUNK):
            rows = slice(r, r + ROW_CHUNK)
            out = h_ref[0, rows] + gate_ref[0] * o_ref[0, rows]
            if final_norm:
                out = _rms(out, fg_ref[...])
            o_ref[0, rows] = out


def _mlp(h, g, sh, sc, gate, w1, w2, layer, final_g, final_norm, tm=512, tf=512):
    B, S, D = h.shape
    F = w1.shape[2]
    vec = pl.BlockSpec((1, D), lambda b, i, f: (0, 0))
    bvec = pl.BlockSpec((1, 1, D), lambda b, i, f: (b, 0, 0))
    return pl.pallas_call(
        functools.partial(_mlp_kernel, final_norm=final_norm),
        grid=(B, S // tm, F // tf),
        in_specs=[pl.BlockSpec((1, tm, D), lambda b, i, f: (b, i, 0), pipeline_mode=pl.Buffered(1)),
                  vec, bvec, bvec, bvec,
                  pl.BlockSpec((None, D, tf), lambda b, i, f: (layer, 0, f)),
                  pl.BlockSpec((None, tf, D), lambda b, i, f: (layer, f, 0)),
                  vec],
        out_specs=pl.BlockSpec((1, tm, D), lambda b, i, f: (b, i, 0)),
        out_shape=jax.ShapeDtypeStruct((B, S, D), F32),
        scratch_shapes=[pltpu.VMEM((tm, D), BF16)],
        compiler_params=_params(("arbitrary", "arbitrary", "arbitrary")),
        name="mlp",
    )(h, g, sh, sc, gate, w1, w2, final_g)


def _sg_in_kernel(h_ref, g_ref, sh_ref, sc_ref, w_ref, z_ref, mu_ref, rstd_ref, a_ref, s1_ref, s2_ref,
                  *, n_half, width):
    n = pl.program_id(2)

    @pl.when(n == 0)
    def _():
        for r in range(0, a_ref.shape[0], ROW_CHUNK):
            rows = slice(r, r + ROW_CHUNK)
            a_ref[rows] = _rms_mod(h_ref[0, rows], g_ref[...], sh_ref[0], sc_ref[0]).astype(BF16)
        s1_ref[...] = jnp.zeros(s1_ref.shape, F32)
        s2_ref[...] = jnp.zeros(s2_ref.shape, F32)

    half = (n >= n_half).astype(jnp.int32)
    for r in range(0, a_ref.shape[0], DOT_ROWS):
        rows = slice(r, r + DOT_ROWS)
        x = jnp.dot(a_ref[rows], w_ref[...], preferred_element_type=F32)
        z = 0.5 * x * (1.0 + lax.erf(x * math.sqrt(0.5)))
        z_ref[0, rows] = z
        zc = [z[:, j * LANES:(j + 1) * LANES] for j in range(z.shape[1] // LANES)]
        s1_ref[half, rows] += functools.reduce(lambda a, b: a + b, zc)
        s2_ref[half, rows] += functools.reduce(lambda a, b: a + b, [c * c for c in zc])

    @pl.when(n == pl.num_programs(2) - 1)
    def _():
        mu = jnp.sum(s1_ref[1], axis=-1, keepdims=True) / width
        var = jnp.sum(s2_ref[1], axis=-1, keepdims=True) / width - mu * mu
        mu_ref[0] = jnp.broadcast_to(mu, mu_ref.shape[1:])
        rstd_ref[0] = jnp.broadcast_to(lax.rsqrt(var + LN_EPS), rstd_ref.shape[1:])


def _sg_in(h, g, sh, sc, w_in, tm=512, tn=1024):
    B, S, D = h.shape
    N = w_in.shape[1]
    width = N // 2
    vec = pl.BlockSpec((1, D), lambda b, i, n: (0, 0))
    bvec = pl.BlockSpec((1, 1, D), lambda b, i, n: (b, 0, 0))
    stat = pl.BlockSpec((1, tm, LANES), lambda b, i, n: (b, i, 0))
    return pl.pallas_call(
        functools.partial(_sg_in_kernel, n_half=width // tn, width=float(width)),
        grid=(B, S // tm, N // tn),
        in_specs=[pl.BlockSpec((1, tm, D), lambda b, i, n: (b, i, 0), pipeline_mode=pl.Buffered(1)),
                  vec, bvec, bvec,
                  pl.BlockSpec((D, tn), lambda b, i, n: (0, n))],
        out_specs=[pl.BlockSpec((1, tm, tn), lambda b, i, n: (b, i, n)), stat, stat],
        out_shape=[jax.ShapeDtypeStruct((B, S, N), F32),
                   jax.ShapeDtypeStruct((B, S, LANES), F32),
                   jax.ShapeDtypeStruct((B, S, LANES), F32)],
        scratch_shapes=[pltpu.VMEM((tm, D), BF16), pltpu.VMEM((2, tm, LANES), F32),
                        pltpu.VMEM((2, tm, LANES), F32)],
        compiler_params=_params(("arbitrary", "arbitrary", "arbitrary")),
        name="sg_in",
    )(h, g, sh, sc, w_in)


def _sg_out_kernel(u_ref, v_ref, mu_ref, rstd_ref, lng_ref, lnb_ref, ws_ref, bs_ref, w_ref,
                   h_ref, gate_ref, o_ref, gated_ref, *, groups_per_step, group_dim):
    kk = pl.program_id(2)
    tm = u_ref.shape[1]
    T = SG_CHUNK

    @pl.when(kk == 0)
    def _():
        o_ref[0] = jnp.zeros(o_ref.shape[1:], F32)

    row = lax.broadcasted_iota(jnp.int32, (T, T), 0)
    col = lax.broadcasted_iota(jnp.int32, (T, T), 1)
    reps = group_dim // LANES
    ws = [jnp.where(col <= row, ws_ref[gg], 0.0).astype(BF16) for gg in range(groups_per_step)]
    for r in range(0, tm, DOT_ROWS):
        rows = slice(r, r + DOT_ROWS)
        for gg in range(groups_per_step):
            bias = jnp.concatenate([bs_ref[gg]] * reps, axis=1)
            c0, c1 = gg * group_dim, (gg + 1) * group_dim
            for r0 in range(r, r + DOT_ROWS, T):
                r1 = r0 + T
                mu = mu_ref[0, r0:r1, 0:1]
                rstd = rstd_ref[0, r0:r1, 0:1]
                vln = (v_ref[0, r0:r1, c0:c1] - mu) * rstd * lng_ref[:, c0:c1] + lnb_ref[:, c0:c1]
                mixed = jnp.dot(ws[gg], vln.astype(BF16), preferred_element_type=F32) + bias
                gated_ref[r0:r1, c0:c1] = (u_ref[0, r0:r1, c0:c1] * mixed).astype(BF16)
        for n in range(0, o_ref.shape[2], COL_CHUNK):
            cols = slice(n, n + COL_CHUNK)
            o_ref[0, rows, cols] += jnp.dot(gated_ref[rows], w_ref[:, cols], preferred_element_type=F32)

    @pl.when(kk == pl.num_programs(2) - 1)
    def _():
        for r in range(0, tm, ROW_CHUNK):
            rows = slice(r, r + ROW_CHUNK)
            o_ref[0, rows] = h_ref[0, rows] + gate_ref[0] * o_ref[0, rows]


def _sg_out(z, mu, rstd, ln_g, ln_b, w_s, bs_rep, w_out, h, gate, tm=512, groups_per_step=2):
    B, S, D = h.shape
    W = w_out.shape[0]
    group_dim = W // SG_GROUPS
    kc = groups_per_step * group_dim
    n_k = W // kc
    stat = pl.BlockSpec((1, tm, LANES), lambda b, i, k: (b, i, 0))
    return pl.pallas_call(
        functools.partial(_sg_out_kernel, groups_per_step=groups_per_step, group_dim=group_dim),
        grid=(B, S // tm, n_k),
        in_specs=[pl.BlockSpec((1, tm, kc), lambda b, i, k: (b, i, k)),
                  pl.BlockSpec((1, tm, kc), lambda b, i, k: (b, i, k + n_k)),
                  stat, stat,
                  pl.BlockSpec((1, kc), lambda b, i, k: (0, k)),
                  pl.BlockSpec((1, kc), lambda b, i, k: (0, k)),
                  pl.BlockSpec((groups_per_step, SG_CHUNK, SG_CHUNK), lambda b, i, k: (k, 0, 0)),
                  pl.BlockSpec((groups_per_step, SG_CHUNK, LANES), lambda b, i, k: (k, 0, 0)),
                  pl.BlockSpec((kc, D), lambda b, i, k: (k, 0)),
                  pl.BlockSpec((1, tm, D), lambda b, i, k: (b, i, 0), pipeline_mode=pl.Buffered(1)),
                  pl.BlockSpec((1, 1, D), lambda b, i, k: (b, 0, 0))],
        out_specs=pl.BlockSpec((1, tm, D), lambda b, i, k: (b, i, 0)),
        out_shape=jax.ShapeDtypeStruct((B, S, D), F32),
        scratch_shapes=[pltpu.VMEM((tm, kc), BF16)],
        compiler_params=_params(("arbitrary", "arbitrary", "arbitrary")),
        name="sg_out",
    )(z, z, mu, rstd, ln_g, ln_b, w_s, bs_rep, w_out, h, gate)


def _down_weight_layout(w_down):
    lat = Q_LORA_RANK + KV_LORA_RANK
    half = QK_ROPE_DIM // 2
    rope = w_down[:, lat:lat + QK_ROPE_DIM]
    swapped = jnp.concatenate([rope[:, half:], rope[:, :half]], axis=1)
    return jnp.concatenate([w_down[:, :lat], rope, rope, swapped, swapped], axis=1)


def _uq_weight_layout(w_uq):
    R = w_uq.shape[0]
    G = HEAD_GROUP
    half = QK_ROPE_DIM // 2
    w = w_uq.reshape(R, MLA_HEADS // G, G, QK_NOPE_DIM + QK_ROPE_DIM)
    nope = w[..., :QK_NOPE_DIM].reshape(R, MLA_HEADS // G, G * QK_NOPE_DIM)
    rope = w[..., QK_NOPE_DIM:]
    swapped = jnp.concatenate([rope[..., half:], rope[..., :half]], axis=-1)
    rope = rope.reshape(R, MLA_HEADS // G, G * QK_ROPE_DIM)
    swapped = swapped.reshape(R, MLA_HEADS // G, G * QK_ROPE_DIM)
    return jnp.concatenate([nope, rope, swapped], axis=-1).reshape(R, -1)


def _ukv_weight_layout(w_ukv):
    R = w_ukv.shape[0]
    w = w_ukv.reshape(R, MLA_HEADS, QK_NOPE_DIM + V_HEAD_DIM)
    w_k = w[..., :QK_NOPE_DIM].reshape(R, MLA_HEADS * QK_NOPE_DIM)
    w_vt = w[..., QK_NOPE_DIM:].reshape(R, MLA_HEADS * V_HEAD_DIM).T
    return w_k, w_vt


def kernel(x, c, positions, ada_w, ada_b, norm1_g, norm2_g, mla_w_down, mla_q_norm_g, mla_w_uq,
           mla_kv_norm_g, mla_w_ukv, mla_w_o, sg_w_in, sg_v_norm_g, sg_v_norm_b, sg_w_s, sg_b_s,
           sg_w_out, mlp_w1, mlp_w2, final_norm_g):
    B, S, D = x.shape
    depth = ada_w.shape[0]
    cos, sin = _rope_tables(positions)
    cos = cos.reshape(B, S, LANES)
    sin = sin.reshape(B, S, LANES)
    mod = _ada_mod(c, ada_w, ada_b).reshape(depth, B, N_MOD, 1, D)
    final_g = final_norm_g.reshape(1, D)
    w1_bf = mlp_w1.astype(BF16)
    w2_bf = mlp_w2.astype(BF16)

    h = x
    for i in range(depth):
        sh1, sc1, g1, sh2, sc2, g2 = [mod[i, :, m] for m in range(N_MOD)]
        n1 = norm1_g[i].reshape(1, D)
        j = i // 2
        if i % 2 == 0:
            w_down_r = _down_weight_layout(mla_w_down[j]).astype(BF16)
            w_uq_r = _uq_weight_layout(mla_w_uq[j]).astype(BF16)
            cq, ckv, kr = _mla_down(h, n1, sh1, sc1, w_down_r,
                                    mla_q_norm_g[j].reshape(1, -1), mla_kv_norm_g[j].reshape(1, -1),
                                    cos, sin)
            q = _mla_q(cq, w_uq_r, cos, sin)
            w_k, w_vt = _ukv_weight_layout(mla_w_ukv[j])
            k, vt = _mla_kv(ckv, w_k.astype(BF16), w_vt.astype(BF16), kr)
            o = _mla_attention(q, k, vt)
            h = _proj_residual(o, mla_w_o[j].astype(BF16), h, g1)
        else:
            z, mu, rstd = _sg_in(h, n1, sh1, sc1, sg_w_in[j].astype(BF16))
            bs_rep = jnp.broadcast_to(sg_b_s[j][:, :, None], (SG_GROUPS, SG_CHUNK, LANES))
            h = _sg_out(z, mu, rstd, sg_v_norm_g[j].reshape(1, -1), sg_v_norm_b[j].reshape(1, -1),
                        sg_w_s[j], bs_rep, sg_w_out[j].astype(BF16), h, g1)
        h = _mlp(h, norm2_g[i].reshape(1, D), sh2, sc2, g2, w1_bf, w2_bf, i, final_g,
                 final_norm=(i == depth - 1))
    return h
```

```python
import functools
import math

import jax
import jax.numpy as jnp
import numpy as np
from jax import lax
from jax.experimental import pallas as pl
from jax.experimental.pallas import tpu as pltpu

F32 = jnp.float32
BF16 = jnp.bfloat16

MLA_HEADS = 32
QK_NOPE_DIM = 128
QK_ROPE_DIM = 64
V_HEAD_DIM = 128
Q_LORA_RANK = 1024
KV_LORA_RANK = 512
ROPE_THETA = 10000.0
SG_CHUNK = 128
SG_GROUPS = 32
NORM_EPS = 1e-6
LN_EPS = 1e-5
N_MOD = 6

LANES = 128
QK_PAD_DIM = 256
HEAD_GROUP = 8
VMEM_LIMIT = 56 * 1024 * 1024
ROW_CHUNK = 128
COL_CHUNK = 1024
DOT_ROWS = 256


def _params(semantics):
    return pltpu.CompilerParams(dimension_semantics=semantics, vmem_limit_bytes=VMEM_LIMIT)


def _rms_mod(h, g, sh, sc):
    ms = jnp.mean(h * h, axis=-1, keepdims=True)
    y = h * lax.rsqrt(ms + NORM_EPS) * g
    return y * (1.0 + sc) + sh


def _rms(x, g):
    ms = jnp.mean(x * x, axis=-1, keepdims=True)
    return x * lax.rsqrt(ms + NORM_EPS) * g


def _ada_kernel(c_ref, w_ref, b_ref, o_ref, *, batch, tn):
    w = w_ref[0]
    for b in range(batch):
        c = c_ref[b]
        ca = c / (1.0 + jnp.exp(-c))
        cols = [jnp.sum(w[:, j * LANES:(j + 1) * LANES] * ca, axis=0, keepdims=True)
                for j in range(tn // LANES)]
        o_ref[0, b:b + 1, :] = jnp.concatenate(cols, axis=1) + b_ref[0]


def _ada_mod(c, ada_w, ada_b, tn=512):
    L, D, N = ada_w.shape
    B = c.shape[0]
    c_rep = jnp.broadcast_to(c[:, :, None], (B, D, LANES))
    return pl.pallas_call(
        functools.partial(_ada_kernel, batch=B, tn=tn),
        grid=(L, N // tn),
        in_specs=[
            pl.BlockSpec((B, D, LANES), lambda l, n: (0, 0, 0)),
            pl.BlockSpec((1, D, tn), lambda l, n: (l, 0, n)),
            pl.BlockSpec((1, 1, tn), lambda l, n: (l, 0, n)),
        ],
        out_specs=pl.BlockSpec((1, B, tn), lambda l, n: (l, 0, n)),
        out_shape=jax.ShapeDtypeStruct((L, B, N), F32),
        compiler_params=_params(("arbitrary", "arbitrary")),
        name="ada_mod",
    )(c_rep, ada_w, ada_b.reshape(L, 1, N))


def _rope_kernel(pos_ref, invf_ref, sign_ref, cos_ref, sin_ref):
    ang = pos_ref[...].astype(F32) * invf_ref[...]
    cos_ref[...] = jnp.cos(ang)
    sin_ref[...] = jnp.sin(ang) * sign_ref[...]


def _rope_tables(positions, tm=1024):
    T = positions.size
    half = QK_ROPE_DIM // 2
    inv_freq = jnp.power(ROPE_THETA, -jnp.arange(0, QK_ROPE_DIM, 2, dtype=F32) / QK_ROPE_DIM)
    invf = jnp.tile(inv_freq, LANES // half).reshape(1, LANES)
    sign = jnp.tile(jnp.concatenate([-jnp.ones((half,), F32), jnp.ones((half,), F32)]),
                    LANES // QK_ROPE_DIM).reshape(1, LANES)
    row = pl.BlockSpec((tm, LANES), lambda i: (i, 0))
    const = pl.BlockSpec((1, LANES), lambda i: (0, 0))
    return pl.pallas_call(
        _rope_kernel,
        grid=(T // tm,),
        in_specs=[pl.BlockSpec((tm, 1), lambda i: (i, 0)), const, const],
        out_specs=[row, row],
        out_shape=[jax.ShapeDtypeStruct((T, LANES), F32)] * 2,
        compiler_params=_params(("arbitrary",)),
        name="rope_tables",
    )(positions.reshape(T, 1), invf, sign)


def _down_kernel(h_ref, g_ref, sh_ref, sc_ref, w_ref, qg_ref, kvg_ref, cos_ref, sin_ref,
                 cq_ref, ckv_ref, kr_ref):
    a = _rms_mod(h_ref[0], g_ref[...], sh_ref[0], sc_ref[0])
    down = jnp.dot(a.astype(BF16), w_ref[...], preferred_element_type=F32)
    q_end = Q_LORA_RANK
    kv_end = q_end + KV_LORA_RANK
    cq_ref[0] = _rms(down[:, :q_end], qg_ref[...]).astype(BF16)
    ckv_ref[0] = _rms(down[:, q_end:kv_end], kvg_ref[...]).astype(BF16)
    roped = (down[:, kv_end:kv_end + LANES] * cos_ref[0]
             + down[:, kv_end + LANES:kv_end + 2 * LANES] * sin_ref[0])
    lane = lax.broadcasted_iota(jnp.int32, roped.shape, 1)
    kr_ref[0] = jnp.where(lane < QK_ROPE_DIM, roped, 0.0).astype(BF16)


def _mla_down(h, g, sh, sc, w_down_r, q_g, kv_g, cos, sin, tm=256):
    B, S, D = h.shape
    N = w_down_r.shape[1]
    row = lambda n: pl.BlockSpec((1, tm, n), lambda b, i: (b, i, 0))
    vec = lambda n: pl.BlockSpec((1, n), lambda b, i: (0, 0))
    bvec = pl.BlockSpec((1, 1, D), lambda b, i: (b, 0, 0))
    return pl.pallas_call(
        _down_kernel,
        grid=(B, S // tm),
        in_specs=[row(D), vec(D), bvec, bvec,
                  pl.BlockSpec((D, N), lambda b, i: (0, 0), pipeline_mode=pl.Buffered(1)),
                  vec(Q_LORA_RANK), vec(KV_LORA_RANK), row(LANES), row(LANES)],
        out_specs=[row(Q_LORA_RANK), row(KV_LORA_RANK), row(LANES)],
        out_shape=[jax.ShapeDtypeStruct((B, S, Q_LORA_RANK), BF16),
                   jax.ShapeDtypeStruct((B, S, KV_LORA_RANK), BF16),
                   jax.ShapeDtypeStruct((B, S, LANES), BF16)],
        compiler_params=_params(("arbitrary", "arbitrary")),
        name="mla_down",
    )(h, g, sh, sc, w_down_r, q_g, kv_g, cos, sin)


def _q_kernel(cq_ref, w_ref, cos_ref, sin_ref, q_ref):
    G = HEAD_GROUP
    r = jnp.dot(cq_ref[0], w_ref[...], preferred_element_type=F32)
    cos = cos_ref[0]
    sin = sin_ref[0]
    lane = lax.broadcasted_iota(jnp.int32, cos.shape, 1)
    low = lane < QK_ROPE_DIM
    nope_w = G * QK_NOPE_DIM
    rope_w = G * QK_ROPE_DIM
    for p in range(G // 2):
        x = r[:, nope_w + p * LANES:nope_w + (p + 1) * LANES]
        xs = r[:, nope_w + rope_w + p * LANES:nope_w + rope_w + (p + 1) * LANES]
        roped = x * cos + xs * sin
        halves = (roped, pltpu.roll(roped, QK_ROPE_DIM, 1))
        for k in range(2):
            j = 2 * p + k
            q_ref[0, j, :, :QK_NOPE_DIM] = r[:, j * QK_NOPE_DIM:(j + 1) * QK_NOPE_DIM].astype(BF16)
            q_ref[0, j, :, QK_NOPE_DIM:] = jnp.where(low, halves[k], 0.0).astype(BF16)


def _mla_q(cq, w_uq_r, cos, sin, tm=512):
    B, S, R = cq.shape
    G = HEAD_GROUP
    n_groups = MLA_HEADS // G
    wn = w_uq_r.shape[1] // n_groups
    return pl.pallas_call(
        _q_kernel,
        grid=(B, S // tm, n_groups),
        in_specs=[pl.BlockSpec((1, tm, R), lambda b, i, g: (b, i, 0)),
                  pl.BlockSpec((R, wn), lambda b, i, g: (0, g)),
                  pl.BlockSpec((1, tm, LANES), lambda b, i, g: (b, i, 0)),
                  pl.BlockSpec((1, tm, LANES), lambda b, i, g: (b, i, 0))],
        out_specs=pl.BlockSpec((1, G, tm, QK_PAD_DIM), lambda b, i, g: (b, g, i, 0)),
        out_shape=jax.ShapeDtypeStruct((B, MLA_HEADS, S, QK_PAD_DIM), BF16),
        compiler_params=_params(("arbitrary", "arbitrary", "arbitrary")),
        name="mla_q_up",
    )(cq, w_uq_r, cos, sin)


def _kv_kernel(ckv_ref, wk_ref, wvt_ref, kr_ref, k_ref, vt_ref):
    ckv = ckv_ref[0]
    kn = jnp.dot(ckv, wk_ref[...], preferred_element_type=F32)
    vt = lax.dot_general(wvt_ref[...], ckv, (((1,), (1,)), ((), ())),
                         preferred_element_type=F32)
    kr = kr_ref[0]
    for j in range(HEAD_GROUP):
        k_ref[0, j, :, :QK_NOPE_DIM] = kn[:, j * QK_NOPE_DIM:(j + 1) * QK_NOPE_DIM].astype(BF16)
        k_ref[0, j, :, QK_NOPE_DIM:] = kr
        vt_ref[0, j] = vt[j * V_HEAD_DIM:(j + 1) * V_HEAD_DIM, :].astype(BF16)


def _mla_kv(ckv, w_k, w_vt, kr, tm=512):
    B, S, R = ckv.shape
    G = HEAD_GROUP
    return pl.pallas_call(
        _kv_kernel,
        grid=(B, S // tm, MLA_HEADS // G),
        in_specs=[pl.BlockSpec((1, tm, R), lambda b, i, g: (b, i, 0)),
                  pl.BlockSpec((R, G * QK_NOPE_DIM), lambda b, i, g: (0, g)),
                  pl.BlockSpec((G * V_HEAD_DIM, R), lambda b, i, g: (g, 0)),
                  pl.BlockSpec((1, tm, LANES), lambda b, i, g: (b, i, 0))],
        out_specs=[pl.BlockSpec((1, G, tm, QK_PAD_DIM), lambda b, i, g: (b, g, i, 0)),
                   pl.BlockSpec((1, G, V_HEAD_DIM, tm), lambda b, i, g: (b, g, 0, i))],
        out_shape=[jax.ShapeDtypeStruct((B, MLA_HEADS, S, QK_PAD_DIM), BF16),
                   jax.ShapeDtypeStruct((B, MLA_HEADS, V_HEAD_DIM, S), BF16)],
        compiler_params=_params(("arbitrary", "arbitrary", "arbitrary")),
        name="mla_kv_up",
    )(ckv, w_k, w_vt, kr)


def _attn_kernel(q_ref, k_ref, vt_ref, o_ref, qt_ref, sa_ref, sb_ref, m_ref, l_ref, acc_ref,
                 *, tq, exp2_scale):
    tk = tq // 2
    qi = pl.program_id(2)
    qt_ref[...] = q_ref[0, 0].T
    m_ref[...] = jnp.full(m_ref.shape, -jnp.inf, F32)
    l_ref[...] = jnp.zeros(l_ref.shape, F32)
    acc_ref[...] = jnp.zeros(acc_ref.shape, F32)

    def scores(j, s_ref, q0=0):
        start = pl.multiple_of(j * tk, tk)
        s_ref[:, q0:] = jnp.dot(k_ref[0, 0, pl.ds(start, tk), :], qt_ref[:, q0:],
                                preferred_element_type=F32)

    def update(j, s_ref, masked, q0=0):
        start = pl.multiple_of(j * tk, tk)
        s = s_ref[:, q0:]
        if masked:
            key = lax.broadcasted_iota(jnp.int32, s.shape, 0)
            qry = lax.broadcasted_iota(jnp.int32, s.shape, 1)
            s = jnp.where(key <= qry, s, -jnp.inf)
        m_prev = m_ref[:, q0:]
        m_new = jnp.maximum(m_prev, jnp.max(s, axis=0, keepdims=True))
        alpha = jnp.exp2((m_prev - m_new) * exp2_scale)
        p = jnp.exp2((s - m_new) * exp2_scale)
        l_ref[:, q0:] = alpha * l_ref[:, q0:] + jnp.sum(p, axis=0, keepdims=True)
        pv = jnp.dot(vt_ref[0, 0, :, pl.ds(start, tk)], p.astype(BF16), preferred_element_type=F32)
        acc_ref[:, q0:] = alpha * acc_ref[:, q0:] + pv
        m_ref[:, q0:] = m_new

    scores(0, sa_ref)

    def pair(i, carry):
        j = 2 * i
        scores(j + 1, sb_ref)
        update(j, sa_ref, False)
        scores(j + 2, sa_ref)
        update(j + 1, sb_ref, False)
        return carry

    lax.fori_loop(0, qi, pair, 0)
    scores(2 * qi + 1, sb_ref, q0=tk)
    update(2 * qi, sa_ref, True)
    update(2 * qi + 1, sb_ref, True, q0=tk)

    o_ref[0] = (acc_ref[...] / l_ref[...]).T.astype(o_ref.dtype)


def _mla_attention(q, k, vt, tq=1024):
    B, H, S, _ = q.shape
    scale = (QK_NOPE_DIM + QK_ROPE_DIM) ** -0.5
    return pl.pallas_call(
        functools.partial(_attn_kernel, tq=tq, exp2_scale=scale * math.log2(math.e)),
        grid=(B, H, S // tq),
        in_specs=[pl.BlockSpec((1, 1, tq, QK_PAD_DIM), lambda b, h, i: (b, h, i, 0)),
                  pl.BlockSpec((1, 1, S, QK_PAD_DIM), lambda b, h, i: (b, h, 0, 0)),
                  pl.BlockSpec((1, 1, V_HEAD_DIM, S), lambda b, h, i: (b, h, 0, 0))],
        out_specs=pl.BlockSpec((1, tq, V_HEAD_DIM), lambda b, h, i: (b, i, h)),
        out_shape=jax.ShapeDtypeStruct((B, S, H * V_HEAD_DIM), BF16),
        scratch_shapes=[pltpu.VMEM((QK_PAD_DIM, tq), BF16),
                        pltpu.VMEM((tq // 2, tq), F32), pltpu.VMEM((tq // 2, tq), F32),
                        pltpu.VMEM((1, tq), F32), pltpu.VMEM((1, tq), F32),
                        pltpu.VMEM((V_HEAD_DIM, tq), F32)],
        compiler_params=_params(("arbitrary", "arbitrary", "arbitrary")),
        name="mla_attention",
    )(q, k, vt)


def _proj_res_kernel(a_ref, w_ref, h_ref, g_ref, o_ref):
    r = jnp.dot(a_ref[0], w_ref[...], preferred_element_type=F32)
    o_ref[0] = h_ref[0] + g_ref[0] * r


def _proj_residual(a, w, h, gate, tm=512, tn=1024):
    B, S, K = a.shape
    N = w.shape[1]
    return pl.pallas_call(
        _proj_res_kernel,
        grid=(B, S // tm, N // tn),
        in_specs=[pl.BlockSpec((1, tm, K), lambda b, i, n: (b, i, 0)),
                  pl.BlockSpec((K, tn), lambda b, i, n: (0, n)),
                  pl.BlockSpec((1, tm, tn), lambda b, i, n: (b, i, n)),
                  pl.BlockSpec((1, 1, tn), lambda b, i, n: (b, 0, n))],
        out_specs=pl.BlockSpec((1, tm, tn), lambda b, i, n: (b, i, n)),
        out_shape=jax.ShapeDtypeStruct((B, S, N), F32),
        compiler_params=_params(("arbitrary", "arbitrary", "arbitrary")),
        name="proj_residual",
    )(a, w, h, gate)


def _mlp_kernel(h_ref, g_ref, sh_ref, sc_ref, gate_ref, w1_ref, w2_ref, fg_ref, o_ref, a_ref,
                *, final_norm):
    f = pl.program_id(2)
    tm, D = a_ref.shape

    @pl.when(f == 0)
    def _():
        for r in range(0, tm, ROW_CHUNK):
            rows = slice(r, r + ROW_CHUNK)
            a_ref[rows] = _rms_mod(h_ref[0, rows], g_ref[...], sh_ref[0], sc_ref[0]).astype(BF16)
        o_ref[0] = jnp.zeros((tm, D), F32)

    hid = jnp.dot(a_ref[...], w1_ref[...], preferred_element_type=F32)
    hid = jnp.square(jnp.maximum(hid, 0.0)).astype(BF16)
    for n in range(0, D, COL_CHUNK):
        cols = slice(n, n + COL_CHUNK)
        o_ref[0, :, cols] += jnp.dot(hid, w2_ref[:, cols], preferred_element_type=F32)

    @pl.when(f == pl.num_programs(2) - 1)
    def _():
        for r in range(0, tm, ROW_CHUNK):
            rows = slice(r, r + ROW_CHUNK)
            out = h_ref[0, rows] + gate_ref[0] * o_ref[0, rows]
            if final_norm:
                out = _rms(out, fg_ref[...])
            o_ref[0, rows] = out


def _mlp(h, g, sh, sc, gate, w1, w2, layer, final_g, final_norm, tm=512, tf=512):
    B, S, D = h.shape
    F = w1.shape[2]
    vec = pl.BlockSpec((1, D), lambda b, i, f: (0, 0))
    bvec = pl.BlockSpec((1, 1, D), lambda b, i, f: (b, 0, 0))
    return pl.pallas_call(
        functools.partial(_mlp_kernel, final_norm=final_norm),
        grid=(B, S // tm, F // tf),
        in_specs=[pl.BlockSpec((1, tm, D), lambda b, i, f: (b, i, 0), pipeline_mode=pl.Buffered(1)),
                  vec, bvec, bvec, bvec,
                  pl.BlockSpec((None, D, tf), lambda b, i, f: (layer, 0, f)),
                  pl.BlockSpec((None, tf, D), lambda b, i, f: (layer, f, 0)),
                  vec],
        out_specs=pl.BlockSpec((1, tm, D), lambda b, i, f: (b, i, 0)),
        out_shape=jax.ShapeDtypeStruct((B, S, D), F32),
        scratch_shapes=[pltpu.VMEM((tm, D), BF16)],
        compiler_params=_params(("arbitrary", "arbitrary", "arbitrary")),
        name="mlp",
    )(h, g, sh, sc, gate, w1, w2, final_g)


def _sg_in_kernel(h_ref, g_ref, sh_ref, sc_ref, w_ref, z_ref, mu_ref, rstd_ref, a_ref, s1_ref, s2_ref,
                  *, n_half, width):
    n = pl.program_id(2)

    @pl.when(n == 0)
    def _():
        for r in range(0, a_ref.shape[0], ROW_CHUNK):
            rows = slice(r, r + ROW_CHUNK)
            a_ref[rows] = _rms_mod(h_ref[0, rows], g_ref[...], sh_ref[0], sc_ref[0]).astype(BF16)
        s1_ref[...] = jnp.zeros(s1_ref.shape, F32)
        s2_ref[...] = jnp.zeros(s2_ref.shape, F32)

    half = (n >= n_half).astype(jnp.int32)
    for r in range(0, a_ref.shape[0], DOT_ROWS):
        rows = slice(r, r + DOT_ROWS)
        x = jnp.dot(a_ref[rows], w_ref[...], preferred_element_type=F32)
        z = 0.5 * x * (1.0 + lax.erf(x * math.sqrt(0.5)))
        z_ref[0, rows] = z
        zc = [z[:, j * LANES:(j + 1) * LANES] for j in range(z.shape[1] // LANES)]
        s1_ref[half, rows] += functools.reduce(lambda a, b: a + b, zc)
        s2_ref[half, rows] += functools.reduce(lambda a, b: a + b, [c * c for c in zc])

    @pl.when(n == pl.num_programs(2) - 1)
    def _():
        mu = jnp.sum(s1_ref[1], axis=-1, keepdims=True) / width
        var = jnp.sum(s2_ref[1], axis=-1, keepdims=True) / width - mu * mu
        mu_ref[0] = jnp.broadcast_to(mu, mu_ref.shape[1:])
        rstd_ref[0] = jnp.broadcast_to(lax.rsqrt(var + LN_EPS), rstd_ref.shape[1:])


def _sg_in(h, g, sh, sc, w_in, tm=1024, tn=512):
    B, S, D = h.shape
    N = w_in.shape[1]
    width = N // 2
    vec = pl.BlockSpec((1, D), lambda b, i, n: (0, 0))
    bvec = pl.BlockSpec((1, 1, D), lambda b, i, n: (b, 0, 0))
    stat = pl.BlockSpec((1, tm, LANES), lambda b, i, n: (b, i, 0))
    return pl.pallas_call(
        functools.partial(_sg_in_kernel, n_half=width // tn, width=float(width)),
        grid=(B, S // tm, N // tn),
        in_specs=[pl.BlockSpec((1, tm, D), lambda b, i, n: (b, i, 0), pipeline_mode=pl.Buffered(1)),
                  vec, bvec, bvec,
                  pl.BlockSpec((D, tn), lambda b, i, n: (0, n))],
        out_specs=[pl.BlockSpec((1, tm, tn), lambda b, i, n: (b, i, n)), stat, stat],
        out_shape=[jax.ShapeDtypeStruct((B, S, N), F32),
                   jax.ShapeDtypeStruct((B, S, LANES), F32),
                   jax.ShapeDtypeStruct((B, S, LANES), F32)],
        scratch_shapes=[pltpu.VMEM((tm, D), BF16), pltpu.VMEM((2, tm, LANES), F32),
                        pltpu.VMEM((2, tm, LANES), F32)],
        compiler_params=_params(("arbitrary", "arbitrary", "arbitrary")),
        name="sg_in",
    )(h, g, sh, sc, w_in)


def _sg_out_kernel(u_ref, v_ref, mu_ref, rstd_ref, lng_ref, lnb_ref, ws_ref, bs_ref, w_ref,
                   h_ref, gate_ref, o_ref, gated_ref, *, groups_per_step, group_dim):
    kk = pl.program_id(2)
    tm = u_ref.shape[1]
    T = SG_CHUNK

    @pl.when(kk == 0)
    def _():
        o_ref[0] = jnp.zeros(o_ref.shape[1:], F32)

    row = lax.broadcasted_iota(jnp.int32, (T, T), 0)
    col = lax.broadcasted_iota(jnp.int32, (T, T), 1)
    reps = group_dim // LANES
    ws = [jnp.where(col <= row, ws_ref[gg], 0.0).astype(BF16) for gg in range(groups_per_step)]
    for r in range(0, tm, DOT_ROWS):
        rows = slice(r, r + DOT_ROWS)
        for gg in range(groups_per_step):
            bias = jnp.concatenate([bs_ref[gg]] * reps, axis=1)
            c0, c1 = gg * group_dim, (gg + 1) * group_dim
            for r0 in range(r, r + DOT_ROWS, T):
                r1 = r0 + T
                mu = mu_ref[0, r0:r1, 0:1]
                rstd = rstd_ref[0, r0:r1, 0:1]
                vln = (v_ref[0, r0:r1, c0:c1] - mu) * rstd * lng_ref[:, c0:c1] + lnb_ref[:, c0:c1]
                mixed = jnp.dot(ws[gg], vln.astype(BF16), preferred_element_type=F32) + bias
                gated_ref[r0:r1, c0:c1] = (u_ref[0, r0:r1, c0:c1] * mixed).astype(BF16)
        for n in range(0, o_ref.shape[2], COL_CHUNK):
            cols = slice(n, n + COL_CHUNK)
            o_ref[0, rows, cols] += jnp.dot(gated_ref[rows], w_ref[:, cols], preferred_element_type=F32)

    @pl.when(kk == pl.num_programs(2) - 1)
    def _():
        for r in range(0, tm, ROW_CHUNK):
            rows = slice(r, r + ROW_CHUNK)
            o_ref[0, rows] = h_ref[0, rows] + gate_ref[0] * o_ref[0, rows]


def _sg_out(z, mu, rstd, ln_g, ln_b, w_s, bs_rep, w_out, h, gate, tm=512, groups_per_step=2):
    B, S, D = h.shape
    W = w_out.shape[0]
    group_dim = W // SG_GROUPS
    kc = groups_per_step * group_dim
    n_k = W // kc
    stat = pl.BlockSpec((1, tm, LANES), lambda b, i, k: (b, i, 0))
    return pl.pallas_call(
        functools.partial(_sg_out_kernel, groups_per_step=groups_per_step, group_dim=group_dim),
        grid=(B, S // tm, n_k),
        in_specs=[pl.BlockSpec((1, tm, kc), lambda b, i, k: (b, i, k)),
                  pl.BlockSpec((1, tm, kc), lambda b, i, k: (b, i, k + n_k)),
                  stat, stat,
                  pl.BlockSpec((1, kc), lambda b, i, k: (0, k)),
                  pl.BlockSpec((1, kc), lambda b, i, k: (0, k)),
                  pl.BlockSpec((groups_per_step, SG_CHUNK, SG_CHUNK), lambda b, i, k: (k, 0, 0)),
                  pl.BlockSpec((groups_per_step, SG_CHUNK, LANES), lambda b, i, k: (k, 0, 0)),
                  pl.BlockSpec((kc, D), lambda b, i, k: (k, 0)),
                  pl.BlockSpec((1, tm, D), lambda b, i, k: (b, i, 0), pipeline_mode=pl.Buffered(1)),
                  pl.BlockSpec((1, 1, D), lambda b, i, k: (b, 0, 0))],
        out_specs=pl.BlockSpec((1, tm, D), lambda b, i, k: (b, i, 0)),
        out_shape=jax.ShapeDtypeStruct((B, S, D), F32),
        scratch_shapes=[pltpu.VMEM((tm, kc), BF16)],
        compiler_params=_params(("arbitrary", "arbitrary", "arbitrary")),
        name="sg_out",
    )(z, z, mu, rstd, ln_g, ln_b, w_s, bs_rep, w_out, h, gate)


def _down_weight_layout(w_down):
    lat = Q_LORA_RANK + KV_LORA_RANK
    half = QK_ROPE_DIM // 2
    rope = w_down[:, lat:lat + QK_ROPE_DIM]
    swapped = jnp.concatenate([rope[:, half:], rope[:, :half]], axis=1)
    return jnp.concatenate([w_down[:, :lat], rope, rope, swapped, swapped], axis=1)


def _uq_weight_layout(w_uq):
    R = w_uq.shape[0]
    G = HEAD_GROUP
    half = QK_ROPE_DIM // 2
    w = w_uq.reshape(R, MLA_HEADS // G, G, QK_NOPE_DIM + QK_ROPE_DIM)
    nope = w[..., :QK_NOPE_DIM].reshape(R, MLA_HEADS // G, G * QK_NOPE_DIM)
    rope = w[..., QK_NOPE_DIM:]
    swapped = jnp.concatenate([rope[..., half:], rope[..., :half]], axis=-1)
    rope = rope.reshape(R, MLA_HEADS // G, G * QK_ROPE_DIM)
    swapped = swapped.reshape(R, MLA_HEADS // G, G * QK_ROPE_DIM)
    return jnp.concatenate([nope, rope, swapped], axis=-1).reshape(R, -1)


def _ukv_weight_layout(w_ukv):
    R = w_ukv.shape[0]
    w = w_ukv.reshape(R, MLA_HEADS, QK_NOPE_DIM + V_HEAD_DIM)
    w_k = w[..., :QK_NOPE_DIM].reshape(R, MLA_HEADS * QK_NOPE_DIM)
    w_vt = w[..., QK_NOPE_DIM:].reshape(R, MLA_HEADS * V_HEAD_DIM).T
    return w_k, w_vt


def kernel(x, c, positions, ada_w, ada_b, norm1_g, norm2_g, mla_w_down, mla_q_norm_g, mla_w_uq,
           mla_kv_norm_g, mla_w_ukv, mla_w_o, sg_w_in, sg_v_norm_g, sg_v_norm_b, sg_w_s, sg_b_s,
           sg_w_out, mlp_w1, mlp_w2, final_norm_g):
    B, S, D = x.shape
    depth = ada_w.shape[0]
    cos, sin = _rope_tables(positions)
    cos = cos.reshape(B, S, LANES)
    sin = sin.reshape(B, S, LANES)
    mod = _ada_mod(c, ada_w, ada_b).reshape(depth, B, N_MOD, 1, D)
    final_g = final_norm_g.reshape(1, D)
    w1_bf = mlp_w1.astype(BF16)
    w2_bf = mlp_w2.astype(BF16)

    h = x
    for i in range(depth):
        sh1, sc1, g1, sh2, sc2, g2 = [mod[i, :, m] for m in range(N_MOD)]
        n1 = norm1_g[i].reshape(1, D)
        j = i // 2
        if i % 2 == 0:
            w_down_r = _down_weight_layout(mla_w_down[j]).astype(BF16)
            w_uq_r = _uq_weight_layout(mla_w_uq[j]).astype(BF16)
            cq, ckv, kr = _mla_down(h, n1, sh1, sc1, w_down_r,
                                    mla_q_norm_g[j].reshape(1, -1), mla_kv_norm_g[j].reshape(1, -1),
                                    cos, sin)
            q = _mla_q(cq, w_uq_r, cos, sin)
            w_k, w_vt = _ukv_weight_layout(mla_w_ukv[j])
            k, vt = _mla_kv(ckv, w_k.astype(BF16), w_vt.astype(BF16), kr)
            o = _mla_attention(q, k, vt)
            h = _proj_residual(o, mla_w_o[j].astype(BF16), h, g1)
        else:
            z, mu, rstd = _sg_in(h, n1, sh1, sc1, sg_w_in[j].astype(BF16))
            bs_rep = jnp.broadcast_to(sg_b_s[j][:, :, None], (SG_GROUPS, SG_CHUNK, LANES))
            h = _sg_out(z, mu, rstd, sg_v_norm_g[j].reshape(1, -1), sg_v_norm_b[j].reshape(1, -1),
                        sg_w_s[j], bs_rep, sg_w_out[j].astype(BF16), h, g1)
        h = _mlp(h, norm2_g[i].reshape(1, D), sh2, sc2, g2, w1_bf, w2_bf, i, final_g,
                 final_norm=(i == depth - 1))
    return h
```

```python
import functools
import math

import jax
import jax.numpy as jnp
import numpy as np
from jax import lax
from jax.experimental import pallas as pl
from jax.experimental.pallas import tpu as pltpu

F32 = jnp.float32
BF16 = jnp.bfloat16

MLA_HEADS = 32
QK_NOPE_DIM = 128
QK_ROPE_DIM = 64
V_HEAD_DIM = 128
Q_LORA_RANK = 1024
KV_LORA_RANK = 512
ROPE_THETA = 10000.0
SG_CHUNK = 128
SG_GROUPS = 32
NORM_EPS = 1e-6
LN_EPS = 1e-5
N_MOD = 6

LANES = 128
QK_PAD_DIM = 256
HEAD_GROUP = 8
VMEM_LIMIT = 56 * 1024 * 1024
ROW_CHUNK = 16
ROW_CHUNK_UNROLL = 4
COL_CHUNK = 1024
DOT_ROWS = 256


def _params(semantics):
    return pltpu.CompilerParams(dimension_semantics=semantics, vmem_limit_bytes=VMEM_LIMIT)


def _rms_mod(h, g, sh, sc):
    ms = jnp.mean(h * h, axis=-1, keepdims=True)
    y = h * lax.rsqrt(ms + NORM_EPS) * g
    return y * (1.0 + sc) + sh


def _rms(x, g):
    ms = jnp.mean(x * x, axis=-1, keepdims=True)
    return x * lax.rsqrt(ms + NORM_EPS) * g


def _for_row_chunks(n_rows, fn):
    def body(i, carry):
        fn(pl.ds(pl.multiple_of(i * ROW_CHUNK, ROW_CHUNK), ROW_CHUNK))
        return carry
    lax.fori_loop(0, n_rows // ROW_CHUNK, body, 0, unroll=ROW_CHUNK_UNROLL)


def _ada_kernel(c_ref, w_ref, b_ref, o_ref, *, batch, tn):
    w = w_ref[0]
    for b in range(batch):
        c = c_ref[b]
        ca = c / (1.0 + jnp.exp(-c))
        cols = [jnp.sum(w[:, j * LANES:(j + 1) * LANES] * ca, axis=0, keepdims=True)
                for j in range(tn // LANES)]
        o_ref[0, b:b + 1, :] = jnp.concatenate(cols, axis=1) + b_ref[0]


def _ada_mod(c, ada_w, ada_b, tn=512):
    L, D, N = ada_w.shape
    B = c.shape[0]
    c_rep = jnp.broadcast_to(c[:, :, None], (B, D, LANES))
    return pl.pallas_call(
        functools.partial(_ada_kernel, batch=B, tn=tn),
        grid=(L, N // tn),
        in_specs=[
            pl.BlockSpec((B, D, LANES), lambda l, n: (0, 0, 0)),
            pl.BlockSpec((1, D, tn), lambda l, n: (l, 0, n)),
            pl.BlockSpec((1, 1, tn), lambda l, n: (l, 0, n)),
        ],
        out_specs=pl.BlockSpec((1, B, tn), lambda l, n: (l, 0, n)),
        out_shape=jax.ShapeDtypeStruct((L, B, N), F32),
        compiler_params=_params(("arbitrary", "arbitrary")),
        name="ada_mod",
    )(c_rep, ada_w, ada_b.reshape(L, 1, N))


def _rope_kernel(pos_ref, invf_ref, sign_ref, cos_ref, sin_ref):
    ang = pos_ref[...].astype(F32) * invf_ref[...]
    cos_ref[...] = jnp.cos(ang)
    sin_ref[...] = jnp.sin(ang) * sign_ref[...]


def _rope_tables(positions, tm=1024):
    T = positions.size
    half = QK_ROPE_DIM // 2
    inv_freq = jnp.power(ROPE_THETA, -jnp.arange(0, QK_ROPE_DIM, 2, dtype=F32) / QK_ROPE_DIM)
    invf = jnp.tile(inv_freq, LANES // half).reshape(1, LANES)
    sign = jnp.tile(jnp.concatenate([-jnp.ones((half,), F32), jnp.ones((half,), F32)]),
                    LANES // QK_ROPE_DIM).reshape(1, LANES)
    row = pl.BlockSpec((tm, LANES), lambda i: (i, 0))
    const = pl.BlockSpec((1, LANES), lambda i: (0, 0))
    return pl.pallas_call(
        _rope_kernel,
        grid=(T // tm,),
        in_specs=[pl.BlockSpec((tm, 1), lambda i: (i, 0)), const, const],
        out_specs=[row, row],
        out_shape=[jax.ShapeDtypeStruct((T, LANES), F32)] * 2,
        compiler_params=_params(("arbitrary",)),
        name="rope_tables",
    )(positions.reshape(T, 1), invf, sign)


def _down_kernel(h_ref, g_ref, sh_ref, sc_ref, w_ref, qg_ref, kvg_ref, cos_ref, sin_ref,
                 cq_ref, ckv_ref, kr_ref):
    a = _rms_mod(h_ref[0], g_ref[...], sh_ref[0], sc_ref[0])
    down = jnp.dot(a.astype(BF16), w_ref[...], preferred_element_type=F32)
    q_end = Q_LORA_RANK
    kv_end = q_end + KV_LORA_RANK
    cq_ref[0] = _rms(down[:, :q_end], qg_ref[...]).astype(BF16)
    ckv_ref[0] = _rms(down[:, q_end:kv_end], kvg_ref[...]).astype(BF16)
    roped = (down[:, kv_end:kv_end + LANES] * cos_ref[0]
             + down[:, kv_end + LANES:kv_end + 2 * LANES] * sin_ref[0])
    lane = lax.broadcasted_iota(jnp.int32, roped.shape, 1)
    kr_ref[0] = jnp.where(lane < QK_ROPE_DIM, roped, 0.0).astype(BF16)


def _mla_down(h, g, sh, sc, w_down_r, q_g, kv_g, cos, sin, tm=256):
    B, S, D = h.shape
    N = w_down_r.shape[1]
    row = lambda n: pl.BlockSpec((1, tm, n), lambda b, i: (b, i, 0))
    vec = lambda n: pl.BlockSpec((1, n), lambda b, i: (0, 0))
    bvec = pl.BlockSpec((1, 1, D), lambda b, i: (b, 0, 0))
    return pl.pallas_call(
        _down_kernel,
        grid=(B, S // tm),
        in_specs=[row(D), vec(D), bvec, bvec,
                  pl.BlockSpec((D, N), lambda b, i: (0, 0), pipeline_mode=pl.Buffered(1)),
                  vec(Q_LORA_RANK), vec(KV_LORA_RANK), row(LANES), row(LANES)],
        out_specs=[row(Q_LORA_RANK), row(KV_LORA_RANK), row(LANES)],
        out_shape=[jax.ShapeDtypeStruct((B, S, Q_LORA_RANK), BF16),
                   jax.ShapeDtypeStruct((B, S, KV_LORA_RANK), BF16),
                   jax.ShapeDtypeStruct((B, S, LANES), BF16)],
        compiler_params=_params(("arbitrary", "arbitrary")),
        name="mla_down",
    )(h, g, sh, sc, w_down_r, q_g, kv_g, cos, sin)


def _q_kernel(cq_ref, w_ref, cos_ref, sin_ref, q_ref):
    G = HEAD_GROUP
    r = jnp.dot(cq_ref[0], w_ref[...], preferred_element_type=F32)
    cos = cos_ref[0]
    sin = sin_ref[0]
    lane = lax.broadcasted_iota(jnp.int32, cos.shape, 1)
    low = lane < QK_ROPE_DIM
    nope_w = G * QK_NOPE_DIM
    rope_w = G * QK_ROPE_DIM
    for p in range(G // 2):
        x = r[:, nope_w + p * LANES:nope_w + (p + 1) * LANES]
        xs = r[:, nope_w + rope_w + p * LANES:nope_w + rope_w + (p + 1) * LANES]
        roped = x * cos + xs * sin
        halves = (roped, pltpu.roll(roped, QK_ROPE_DIM, 1))
        for k in range(2):
            j = 2 * p + k
            q_ref[0, j, :, :QK_NOPE_DIM] = r[:, j * QK_NOPE_DIM:(j + 1) * QK_NOPE_DIM].astype(BF16)
            q_ref[0, j, :, QK_NOPE_DIM:] = jnp.where(low, halves[k], 0.0).astype(BF16)


def _mla_q(cq, w_uq_r, cos, sin, tm=512):
    B, S, R = cq.shape
    G = HEAD_GROUP
    n_groups = MLA_HEADS // G
    wn = w_uq_r.shape[1] // n_groups
    return pl.pallas_call(
        _q_kernel,
        grid=(B, S // tm, n_groups),
        in_specs=[pl.BlockSpec((1, tm, R), lambda b, i, g: (b, i, 0)),
                  pl.BlockSpec((R, wn), lambda b, i, g: (0, g)),
                  pl.BlockSpec((1, tm, LANES), lambda b, i, g: (b, i, 0)),
                  pl.BlockSpec((1, tm, LANES), lambda b, i, g: (b, i, 0))],
        out_specs=pl.BlockSpec((1, G, tm, QK_PAD_DIM), lambda b, i, g: (b, g, i, 0)),
        out_shape=jax.ShapeDtypeStruct((B, MLA_HEADS, S, QK_PAD_DIM), BF16),
        compiler_params=_params(("arbitrary", "arbitrary", "arbitrary")),
        name="mla_q_up",
    )(cq, w_uq_r, cos, sin)


def _kv_kernel(ckv_ref, wk_ref, wvt_ref, kr_ref, k_ref, vt_ref):
    ckv = ckv_ref[0]
    kn = jnp.dot(ckv, wk_ref[...], preferred_element_type=F32)
    vt = lax.dot_general(wvt_ref[...], ckv, (((1,), (1,)), ((), ())),
                         preferred_element_type=F32)
    kr = kr_ref[0]
    for j in range(HEAD_GROUP):
        k_ref[0, j, :, :QK_NOPE_DIM] = kn[:, j * QK_NOPE_DIM:(j + 1) * QK_NOPE_DIM].astype(BF16)
        k_ref[0, j, :, QK_NOPE_DIM:] = kr
        vt_ref[0, j] = vt[j * V_HEAD_DIM:(j + 1) * V_HEAD_DIM, :].astype(BF16)


def _mla_kv(ckv, w_k, w_vt, kr, tm=512):
    B, S, R = ckv.shape
    G = HEAD_GROUP
    return pl.pallas_call(
        _kv_kernel,
        grid=(B, S // tm, MLA_HEADS // G),
        in_specs=[pl.BlockSpec((1, tm, R), lambda b, i, g: (b, i, 0)),
                  pl.BlockSpec((R, G * QK_NOPE_DIM), lambda b, i, g: (0, g)),
                  pl.BlockSpec((G * V_HEAD_DIM, R), lambda b, i, g: (g, 0)),
                  pl.BlockSpec((1, tm, LANES), lambda b, i, g: (b, i, 0))],
        out_specs=[pl.BlockSpec((1, G, tm, QK_PAD_DIM), lambda b, i, g: (b, g, i, 0)),
                   pl.BlockSpec((1, G, V_HEAD_DIM, tm), lambda b, i, g: (b, g, 0, i))],
        out_shape=[jax.ShapeDtypeStruct((B, MLA_HEADS, S, QK_PAD_DIM), BF16),
                   jax.ShapeDtypeStruct((B, MLA_HEADS, V_HEAD_DIM, S), BF16)],
        compiler_params=_params(("arbitrary", "arbitrary", "arbitrary")),
        name="mla_kv_up",
    )(ckv, w_k, w_vt, kr)


def _attn_kernel(q_ref, k_ref, vt_ref, o_ref, qt_ref, sa_ref, sb_ref, m_ref, l_ref, acc_ref,
                 *, tq, exp2_scale):
    tk = tq // 2
    qi = pl.program_id(2)
    qt_ref[...] = q_ref[0, 0].T
    m_ref[...] = jnp.full(m_ref.shape, -jnp.inf, F32)
    l_ref[...] = jnp.zeros(l_ref.shape, F32)
    acc_ref[...] = jnp.zeros(acc_ref.shape, F32)

    def scores(j, s_ref, q0=0):
        start = pl.multiple_of(j * tk, tk)
        s_ref[:, q0:] = jnp.dot(k_ref[0, 0, pl.ds(start, tk), :], qt_ref[:, q0:],
                                preferred_element_type=F32)

    def update(j, s_ref, masked, q0=0):
        start = pl.multiple_of(j * tk, tk)
        s = s_ref[:, q0:]
        if masked:
            key = lax.broadcasted_iota(jnp.int32, s.shape, 0)
            qry = lax.broadcasted_iota(jnp.int32, s.shape, 1)
            s = jnp.where(key <= qry, s, -jnp.inf)
        m_prev = m_ref[:, q0:]
        m_new = jnp.maximum(m_prev, jnp.max(s, axis=0, keepdims=True))
        alpha = jnp.exp2((m_prev - m_new) * exp2_scale)
        p = jnp.exp2((s - m_new) * exp2_scale)
        l_ref[:, q0:] = alpha * l_ref[:, q0:] + jnp.sum(p, axis=0, keepdims=True)
        pv = jnp.dot(vt_ref[0, 0, :, pl.ds(start, tk)], p.astype(BF16), preferred_element_type=F32)
        acc_ref[:, q0:] = alpha * acc_ref[:, q0:] + pv
        m_ref[:, q0:] = m_new

    scores(0, sa_ref)

    def pair(i, carry):
        j = 2 * i
        scores(j + 1, sb_ref)
        update(j, sa_ref, False)
        scores(j + 2, sa_ref)
        update(j + 1, sb_ref, False)
        return carry

    lax.fori_loop(0, qi, pair, 0)
    scores(2 * qi + 1, sb_ref, q0=tk)
    update(2 * qi, sa_ref, True)
    update(2 * qi + 1, sb_ref, True, q0=tk)

    o_ref[0] = (acc_ref[...] / l_ref[...]).T.astype(o_ref.dtype)


def _mla_attention(q, k, vt, tq=1024):
    B, H, S, _ = q.shape
    scale = (QK_NOPE_DIM + QK_ROPE_DIM) ** -0.5
    return pl.pallas_call(
        functools.partial(_attn_kernel, tq=tq, exp2_scale=scale * math.log2(math.e)),
        grid=(B, H, S // tq),
        in_specs=[pl.BlockSpec((1, 1, tq, QK_PAD_DIM), lambda b, h, i: (b, h, i, 0)),
                  pl.BlockSpec((1, 1, S, QK_PAD_DIM), lambda b, h, i: (b, h, 0, 0)),
                  pl.BlockSpec((1, 1, V_HEAD_DIM, S), lambda b, h, i: (b, h, 0, 0))],
        out_specs=pl.BlockSpec((1, tq, V_HEAD_DIM), lambda b, h, i: (b, i, h)),
        out_shape=jax.ShapeDtypeStruct((B, S, H * V_HEAD_DIM), BF16),
        scratch_shapes=[pltpu.VMEM((QK_PAD_DIM, tq), BF16),
                        pltpu.VMEM((tq // 2, tq), F32), pltpu.VMEM((tq // 2, tq), F32),
                        pltpu.VMEM((1, tq), F32), pltpu.VMEM((1, tq), F32),
                        pltpu.VMEM((V_HEAD_DIM, tq), F32)],
        compiler_params=_params(("arbitrary", "arbitrary", "arbitrary")),
        name="mla_attention",
    )(q, k, vt)


def _proj_res_kernel(a_ref, w_ref, h_ref, g_ref, o_ref):
    r = jnp.dot(a_ref[0], w_ref[...], preferred_element_type=F32)
    o_ref[0] = h_ref[0] + g_ref[0] * r


def _proj_residual(a, w, h, gate, tm=512, tn=1024):
    B, S, K = a.shape
    N = w.shape[1]
    return pl.pallas_call(
        _proj_res_kernel,
        grid=(B, S // tm, N // tn),
        in_specs=[pl.BlockSpec((1, tm, K), lambda b, i, n: (b, i, 0)),
                  pl.BlockSpec((K, tn), lambda b, i, n: (0, n)),
                  pl.BlockSpec((1, tm, tn), lambda b, i, n: (b, i, n)),
                  pl.BlockSpec((1, 1, tn), lambda b, i, n: (b, 0, n))],
        out_specs=pl.BlockSpec((1, tm, tn), lambda b, i, n: (b, i, n)),
        out_shape=jax.ShapeDtypeStruct((B, S, N), F32),
        compiler_params=_params(("arbitrary", "arbitrary", "arbitrary")),
        name="proj_residual",
    )(a, w, h, gate)


def _mlp_kernel(h_ref, g_ref, sh_ref, sc_ref, gate_ref, w1_ref, w2_ref, fg_ref, o_ref, a_ref,
                *, final_norm):
    f = pl.program_id(2)
    tm, D = a_ref.shape

    @pl.when(f == 0)
    def _():
        def norm_rows(rows):
            a_ref[rows] = _rms_mod(h_ref[0, rows], g_ref[...], sh_ref[0], sc_ref[0]).astype(BF16)
        _for_row_chunks(tm, norm_rows)
        o_ref[0] = jnp.zeros((tm, D), F32)

    hid = jnp.dot(a_ref[...], w1_ref[...], preferred_element_type=F32)
    hid = jnp.square(jnp.maximum(hid, 0.0)).astype(BF16)
    for n in range(0, D, COL_CHUNK):
        cols = slice(n, n + COL_CHUNK)
        o_ref[0, :, cols] += jnp.dot(hid, w2_ref[:, cols], preferred_element_type=F32)

    @pl.when(f == pl.num_programs(2) - 1)
    def _():
        def residual_rows(rows):
            out = h_ref[0, rows] + gate_ref[0] * o_ref[0, rows]
            if final_norm:
                out = _rms(out, fg_ref[...])
            o_ref[0, rows] = out
        _for_row_chunks(tm, residual_rows)


def _mlp(h, g, sh, sc, gate, w1, w2, layer, final_g, final_norm, tm=512, tf=512):
    B, S, D = h.shape
    F = w1.shape[2]
    vec = pl.BlockSpec((1, D), lambda b, i, f: (0, 0))
    bvec = pl.BlockSpec((1, 1, D), lambda b, i, f: (b, 0, 0))
    return pl.pallas_call(
        functools.partial(_mlp_kernel, final_norm=final_norm),
        grid=(B, S // tm, F // tf),
        in_specs=[pl.BlockSpec((1, tm, D), lambda b, i, f: (b, i, 0), pipeline_mode=pl.Buffered(1)),
                  vec, bvec, bvec, bvec,
                  pl.BlockSpec((None, D, tf), lambda b, i, f: (layer, 0, f)),
                  pl.BlockSpec((None, tf, D), lambda b, i, f: (layer, f, 0)),
                  vec],
        out_specs=pl.BlockSpec((1, tm, D), lambda b, i, f: (b, i, 0)),
        out_shape=jax.ShapeDtypeStruct((B, S, D), F32),
        scratch_shapes=[pltpu.VMEM((tm, D), BF16)],
        compiler_params=_params(("arbitrary", "arbitrary", "arbitrary")),
        name="mlp",
    )(h, g, sh, sc, gate, w1, w2, final_g)


def _sg_in_kernel(h_ref, g_ref, sh_ref, sc_ref, w_ref, z_ref, mu_ref, rstd_ref, a_ref, x_ref, s1_ref, s2_ref,
                  *, n_tiles, n_half, width):
    n = pl.program_id(2)
    tm = a_ref.shape[0]
    slot = lax.rem(n, 2)

    @pl.when(n == 0)
    def _():
        def norm_rows(rows):
            a_ref[rows] = _rms_mod(h_ref[0, rows], g_ref[...], sh_ref[0], sc_ref[0]).astype(BF16)
        _for_row_chunks(tm, norm_rows)
        s1_ref[...] = jnp.zeros(s1_ref.shape, F32)
        s2_ref[...] = jnp.zeros(s2_ref.shape, F32)
        x_ref[1] = jnp.zeros(x_ref.shape[1:], F32)

    def finish_previous():
        half = (n > n_half).astype(jnp.int32)
        for r in range(0, tm, DOT_ROWS):
            rows = slice(r, r + DOT_ROWS)
            x = x_ref[1 - slot, rows]
            z = 0.5 * x * (1.0 + lax.erf(x * math.sqrt(0.5)))
            z_ref[0, rows] = z.astype(z_ref.dtype)
            zc = [z[:, j * LANES:(j + 1) * LANES] for j in range(z.shape[1] // LANES)]
            s1_ref[half, rows] += functools.reduce(lambda a, b: a + b, zc)
            s2_ref[half, rows] += functools.reduce(lambda a, b: a + b, [c * c for c in zc])

    @pl.when(n < n_tiles)
    def _():
        finish_previous()
        for r in range(0, tm, DOT_ROWS):
            rows = slice(r, r + DOT_ROWS)
            x_ref[slot, rows] = jnp.dot(a_ref[rows], w_ref[...], preferred_element_type=F32)

    @pl.when(n == n_tiles)
    def _():
        finish_previous()
        mu = jnp.sum(s1_ref[1], axis=-1, keepdims=True) / width
        var = jnp.sum(s2_ref[1], axis=-1, keepdims=True) / width - mu * mu
        mu_ref[0] = jnp.broadcast_to(mu, mu_ref.shape[1:])
        rstd_ref[0] = jnp.broadcast_to(lax.rsqrt(var + LN_EPS), rstd_ref.shape[1:])


def _sg_in(h, g, sh, sc, w_in, tm=1024, tn=512):
    B, S, D = h.shape
    N = w_in.shape[1]
    width = N // 2
    n_tiles = N // tn
    vec = pl.BlockSpec((1, D), lambda b, i, n: (0, 0))
    bvec = pl.BlockSpec((1, 1, D), lambda b, i, n: (b, 0, 0))
    stat = pl.BlockSpec((1, tm, LANES), lambda b, i, n: (b, i, 0))
    return pl.pallas_call(
        functools.partial(_sg_in_kernel, n_tiles=n_tiles, n_half=width // tn, width=float(width)),
        grid=(B, S // tm, n_tiles + 1),
        in_specs=[pl.BlockSpec((1, tm, D), lambda b, i, n: (b, i, 0), pipeline_mode=pl.Buffered(1)),
                  vec, bvec, bvec,
                  pl.BlockSpec((D, tn), lambda b, i, n: (0, jnp.minimum(n, n_tiles - 1)))],
        out_specs=[pl.BlockSpec((1, tm, tn), lambda b, i, n: (b, i, jnp.maximum(n - 1, 0))), stat, stat],
        out_shape=[jax.ShapeDtypeStruct((B, S, N), BF16),
                   jax.ShapeDtypeStruct((B, S, LANES), F32),
                   jax.ShapeDtypeStruct((B, S, LANES), F32)],
        scratch_shapes=[pltpu.VMEM((tm, D), BF16), pltpu.VMEM((2, tm, tn), F32),
                        pltpu.VMEM((2, tm, LANES), F32), pltpu.VMEM((2, tm, LANES), F32)],
        compiler_params=_params(("arbitrary", "arbitrary", "arbitrary")),
        name="sg_in",
    )(h, g, sh, sc, w_in)


def _sg_out_kernel(u_ref, v_ref, mu_ref, rstd_ref, lng_ref, lnb_ref, ws_ref, bs_ref, w_ref,
                   h_ref, gate_ref, o_ref, gated_ref, *, groups_per_step, group_dim):
    kk = pl.program_id(2)
    tm = u_ref.shape[1]
    T = SG_CHUNK

    @pl.when(kk == 0)
    def _():
        o_ref[0] = jnp.zeros(o_ref.shape[1:], F32)

    row = lax.broadcasted_iota(jnp.int32, (T, T), 0)
    col = lax.broadcasted_iota(jnp.int32, (T, T), 1)
    reps = group_dim // LANES
    ws = [jnp.where(col <= row, ws_ref[gg], 0.0).astype(BF16) for gg in range(groups_per_step)]
    for r in range(0, tm, DOT_ROWS):
        rows = slice(r, r + DOT_ROWS)
        for gg in range(groups_per_step):
            bias = jnp.concatenate([bs_ref[gg]] * reps, axis=1)
            c0, c1 = gg * group_dim, (gg + 1) * group_dim
            for r0 in range(r, r + DOT_ROWS, T):
                r1 = r0 + T
                mu = mu_ref[0, r0:r1, 0:1]
                rstd = rstd_ref[0, r0:r1, 0:1]
                v = v_ref[0, r0:r1, c0:c1].astype(F32)
                vln = (v - mu) * rstd * lng_ref[:, c0:c1] + lnb_ref[:, c0:c1]
                mixed = jnp.dot(ws[gg], vln.astype(BF16), preferred_element_type=F32) + bias
                gated_ref[r0:r1, c0:c1] = (u_ref[0, r0:r1, c0:c1].astype(F32) * mixed).astype(BF16)
        for n in range(0, o_ref.shape[2], COL_CHUNK):
            cols = slice(n, n + COL_CHUNK)
            o_ref[0, rows, cols] += jnp.dot(gated_ref[rows], w_ref[:, cols], preferred_element_type=F32)

    @pl.when(kk == pl.num_programs(2) - 1)
    def _():
        def residual_rows(rows):
            o_ref[0, rows] = h_ref[0, rows] + gate_ref[0] * o_ref[0, rows]
        _for_row_chunks(tm, residual_rows)


def _sg_out(z, mu, rstd, ln_g, ln_b, w_s, bs_rep, w_out, h, gate, tm=512, groups_per_step=2):
    B, S, D = h.shape
    W = w_out.shape[0]
    group_dim = W // SG_GROUPS
    kc = groups_per_step * group_dim
    n_k = W // kc
    stat = pl.BlockSpec((1, tm, LANES), lambda b, i, k: (b, i, 0))
    return pl.pallas_call(
        functools.partial(_sg_out_kernel, groups_per_step=groups_per_step, group_dim=group_dim),
        grid=(B, S // tm, n_k),
        in_specs=[pl.BlockSpec((1, tm, kc), lambda b, i, k: (b, i, k)),
                  pl.BlockSpec((1, tm, kc), lambda b, i, k: (b, i, k + n_k)),
                  stat, stat,
                  pl.BlockSpec((1, kc), lambda b, i, k: (0, k)),
                  pl.BlockSpec((1, kc), lambda b, i, k: (0, k)),
                  pl.BlockSpec((groups_per_step, SG_CHUNK, SG_CHUNK), lambda b, i, k: (k, 0, 0)),
                  pl.BlockSpec((groups_per_step, SG_CHUNK, LANES), lambda b, i, k: (k, 0, 0)),
                  pl.BlockSpec((kc, D), lambda b, i, k: (k, 0)),
                  pl.BlockSpec((1, tm, D), lambda b, i, k: (b, i, 0), pipeline_mode=pl.Buffered(1)),
                  pl.BlockSpec((1, 1, D), lambda b, i, k: (b, 0, 0))],
        out_specs=pl.BlockSpec((1, tm, D), lambda b, i, k: (b, i, 0)),
        out_shape=jax.ShapeDtypeStruct((B, S, D), F32),
        scratch_shapes=[pltpu.VMEM((tm, kc), BF16)],
        compiler_params=_params(("arbitrary", "arbitrary", "arbitrary")),
        name="sg_out",
    )(z, z, mu, rstd, ln_g, ln_b, w_s, bs_rep, w_out, h, gate)


def _down_weight_layout(w_down):
    lat = Q_LORA_RANK + KV_LORA_RANK
    half = QK_ROPE_DIM // 2
    rope = w_down[:, lat:lat + QK_ROPE_DIM]
    swapped = jnp.concatenate([rope[:, half:], rope[:, :half]], axis=1)
    return jnp.concatenate([w_down[:, :lat], rope, rope, swapped, swapped], axis=1)


def _uq_weight_layout(w_uq):
    R = w_uq.shape[0]
    G = HEAD_GROUP
    half = QK_ROPE_DIM // 2
    w = w_uq.reshape(R, MLA_HEADS // G, G, QK_NOPE_DIM + QK_ROPE_DIM)
    nope = w[..., :QK_NOPE_DIM].reshape(R, MLA_HEADS // G, G * QK_NOPE_DIM)
    rope = w[..., QK_NOPE_DIM:]
    swapped = jnp.concatenate([rope[..., half:], rope[..., :half]], axis=-1)
    rope = rope.reshape(R, MLA_HEADS // G, G * QK_ROPE_DIM)
    swapped = swapped.reshape(R, MLA_HEADS // G, G * QK_ROPE_DIM)
    return jnp.concatenate([nope, rope, swapped], axis=-1).reshape(R, -1)


def _ukv_weight_layout(w_ukv):
    R = w_ukv.shape[0]
    w = w_ukv.reshape(R, MLA_HEADS, QK_NOPE_DIM + V_HEAD_DIM)
    w_k = w[..., :QK_NOPE_DIM].reshape(R, MLA_HEADS * QK_NOPE_DIM)
    w_vt = w[..., QK_NOPE_DIM:].reshape(R, MLA_HEADS * V_HEAD_DIM).T
    return w_k, w_vt


def kernel(x, c, positions, ada_w, ada_b, norm1_g, norm2_g, mla_w_down, mla_q_norm_g, mla_w_uq,
           mla_kv_norm_g, mla_w_ukv, mla_w_o, sg_w_in, sg_v_norm_g, sg_v_norm_b, sg_w_s, sg_b_s,
           sg_w_out, mlp_w1, mlp_w2, final_norm_g):
    B, S, D = x.shape
    depth = ada_w.shape[0]
    cos, sin = _rope_tables(positions)
    cos = cos.reshape(B, S, LANES)
    sin = sin.reshape(B, S, LANES)
    mod = _ada_mod(c, ada_w, ada_b).reshape(depth, B, N_MOD, 1, D)
    final_g = final_norm_g.reshape(1, D)
    w1_bf = mlp_w1.astype(BF16)
    w2_bf = mlp_w2.astype(BF16)

    h = x
    for i in range(depth):
        sh1, sc1, g1, sh2, sc2, g2 = [mod[i, :, m] for m in range(N_MOD)]
        n1 = norm1_g[i].reshape(1, D)
        j = i // 2
        if i % 2 == 0:
            w_down_r = _down_weight_layout(mla_w_down[j]).astype(BF16)
            w_uq_r = _uq_weight_layout(mla_w_uq[j]).astype(BF16)
            cq, ckv, kr = _mla_down(h, n1, sh1, sc1, w_down_r,
                                    mla_q_norm_g[j].reshape(1, -1), mla_kv_norm_g[j].reshape(1, -1),
                                    cos, sin)
            q = _mla_q(cq, w_uq_r, cos, sin)
            w_k, w_vt = _ukv_weight_layout(mla_w_ukv[j])
            k, vt = _mla_kv(ckv, w_k.astype(BF16), w_vt.astype(BF16), kr)
            o = _mla_attention(q, k, vt)
            h = _proj_residual(o, mla_w_o[j].astype(BF16), h, g1)
        else:
            z, mu, rstd = _sg_in(h, n1, sh1, sc1, sg_w_in[j].astype(BF16))
            bs_rep = jnp.broadcast_to(sg_b_s[j][:, :, None], (SG_GROUPS, SG_CHUNK, LANES))
            h = _sg_out(z, mu, rstd, sg_v_norm_g[j].reshape(1, -1), sg_v_norm_b[j].reshape(1, -1),
                        sg_w_s[j], bs_rep, sg_w_out[j].astype(BF16), h, g1)
        h = _mlp(h, norm2_g[i].reshape(1, D), sh2, sc2, g2, w1_bf, w2_bf, i, final_g,
                 final_norm=(i == depth - 1))
    return h
```

```python
import functools
import math

import jax
import jax.numpy as jnp
from jax import lax
from jax.experimental import pallas as pl
from jax.experimental.pallas import tpu as pltpu

F32 = jnp.float32
BF16 = jnp.bfloat16

MLA_HEADS = 32
QK_NOPE_DIM = 128
QK_ROPE_DIM = 64
V_HEAD_DIM = 128
Q_LORA_RANK = 1024
KV_LORA_RANK = 512
ROPE_THETA = 10000.0
SG_CHUNK = 128
SG_GROUPS = 32
NORM_EPS = 1e-6
LN_EPS = 1e-5
N_MOD = 6

LANES = 128
QK_PAD_DIM = 256
HEAD_GROUP = 8
QK_EXP2_SCALE = (QK_NOPE_DIM + QK_ROPE_DIM) ** -0.5 * math.log2(math.e)
VMEM_LIMIT = 56 * 1024 * 1024
SG_IN_VMEM_LIMIT = 60 * 1024 * 1024
ROW_CHUNK = 128
COL_CHUNK = 1024
DOT_ROWS = 256


def _params(semantics, vmem_limit=VMEM_LIMIT):
    return pltpu.CompilerParams(dimension_semantics=semantics, vmem_limit_bytes=vmem_limit)


def _rms_mod(h, g, sh, sc):
    ms = jnp.mean(h * h, axis=-1, keepdims=True)
    y = h * lax.rsqrt(ms + NORM_EPS) * g
    return y * (1.0 + sc) + sh


def _rms(x, g):
    ms = jnp.mean(x * x, axis=-1, keepdims=True)
    return x * lax.rsqrt(ms + NORM_EPS) * g


def _ada_kernel(c_ref, w_ref, b_ref, o_ref, *, batch, tn):
    w = w_ref[0]
    for b in range(batch):
        c = c_ref[b]
        ca = c / (1.0 + jnp.exp(-c))
        cols = [jnp.sum(w[:, j * LANES:(j + 1) * LANES] * ca, axis=0, keepdims=True)
                for j in range(tn // LANES)]
        o_ref[0, b:b + 1, :] = jnp.concatenate(cols, axis=1) + b_ref[0]


def _ada_mod(c, ada_w, ada_b, tn=512):
    L, D, N = ada_w.shape
    B = c.shape[0]
    c_rep = jnp.broadcast_to(c[:, :, None], (B, D, LANES))
    return pl.pallas_call(
        functools.partial(_ada_kernel, batch=B, tn=tn),
        grid=(L, N // tn),
        in_specs=[
            pl.BlockSpec((B, D, LANES), lambda l, n: (0, 0, 0)),
            pl.BlockSpec((1, D, tn), lambda l, n: (l, 0, n)),
            pl.BlockSpec((1, 1, tn), lambda l, n: (l, 0, n)),
        ],
        out_specs=pl.BlockSpec((1, B, tn), lambda l, n: (l, 0, n)),
        out_shape=jax.ShapeDtypeStruct((L, B, N), F32),
        compiler_params=_params(("arbitrary", "arbitrary")),
        name="ada_mod",
    )(c_rep, ada_w, ada_b.reshape(L, 1, N))


def _rope_kernel(pos_ref, invf_ref, sign_ref, cos_ref, sin_ref):
    ang = pos_ref[...].astype(F32) * invf_ref[...]
    cos_ref[...] = jnp.cos(ang)
    sin_ref[...] = jnp.sin(ang) * sign_ref[...]


def _rope_tables(positions, tm=1024):
    T = positions.size
    half = QK_ROPE_DIM // 2
    inv_freq = jnp.power(ROPE_THETA, -jnp.arange(0, QK_ROPE_DIM, 2, dtype=F32) / QK_ROPE_DIM)
    invf = jnp.tile(inv_freq, LANES // half).reshape(1, LANES)
    sign = jnp.tile(jnp.concatenate([-jnp.ones((half,), F32), jnp.ones((half,), F32)]),
                    LANES // QK_ROPE_DIM).reshape(1, LANES)
    row = pl.BlockSpec((tm, LANES), lambda i: (i, 0))
    const = pl.BlockSpec((1, LANES), lambda i: (0, 0))
    return pl.pallas_call(
        _rope_kernel,
        grid=(T // tm,),
        in_specs=[pl.BlockSpec((tm, 1), lambda i: (i, 0)), const, const],
        out_specs=[row, row],
        out_shape=[jax.ShapeDtypeStruct((T, LANES), F32)] * 2,
        compiler_params=_params(("arbitrary",)),
        name="rope_tables",
    )(positions.reshape(T, 1), invf, sign)


def _down_kernel(h_ref, g_ref, sh_ref, sc_ref, w_ref, qg_ref, kvg_ref, cos_ref, sin_ref,
                 cq_ref, ckv_ref, kr_ref):
    a = _rms_mod(h_ref[0], g_ref[...], sh_ref[0], sc_ref[0])
    down = jnp.dot(a.astype(BF16), w_ref[...], preferred_element_type=F32)
    q_end = Q_LORA_RANK
    kv_end = q_end + KV_LORA_RANK
    cq_ref[0] = _rms(down[:, :q_end], qg_ref[...]).astype(BF16)
    ckv_ref[0] = _rms(down[:, q_end:kv_end], kvg_ref[...]).astype(BF16)
    roped = (down[:, kv_end:kv_end + LANES] * cos_ref[0]
             + down[:, kv_end + LANES:kv_end + 2 * LANES] * sin_ref[0])
    lane = lax.broadcasted_iota(jnp.int32, roped.shape, 1)
    kr_ref[0] = jnp.where(lane < QK_ROPE_DIM, roped, 0.0).astype(BF16)


def _mla_down(h, g, sh, sc, w_down_r, q_g, kv_g, cos, sin, tm=256):
    B, S, D = h.shape
    N = w_down_r.shape[1]
    row = lambda n: pl.BlockSpec((1, tm, n), lambda b, i: (b, i, 0))
    vec = lambda n: pl.BlockSpec((1, n), lambda b, i: (0, 0))
    bvec = pl.BlockSpec((1, 1, D), lambda b, i: (b, 0, 0))
    return pl.pallas_call(
        _down_kernel,
        grid=(B, S // tm),
        in_specs=[row(D), vec(D), bvec, bvec,
                  pl.BlockSpec((D, N), lambda b, i: (0, 0), pipeline_mode=pl.Buffered(1)),
                  vec(Q_LORA_RANK), vec(KV_LORA_RANK), row(LANES), row(LANES)],
        out_specs=[row(Q_LORA_RANK), row(KV_LORA_RANK), row(LANES)],
        out_shape=[jax.ShapeDtypeStruct((B, S, Q_LORA_RANK), BF16),
                   jax.ShapeDtypeStruct((B, S, KV_LORA_RANK), BF16),
                   jax.ShapeDtypeStruct((B, S, LANES), BF16)],
        compiler_params=_params(("arbitrary", "arbitrary")),
        name="mla_down",
    )(h, g, sh, sc, w_down_r, q_g, kv_g, cos, sin)


def _q_kernel(cq_ref, w_ref, cos_ref, sin_ref, q_ref):
    G = HEAD_GROUP
    r = jnp.dot(cq_ref[0], w_ref[...], preferred_element_type=F32)
    cos = cos_ref[0]
    sin = sin_ref[0]
    lane = lax.broadcasted_iota(jnp.int32, cos.shape, 1)
    low = lane < QK_ROPE_DIM
    nope_w = G * QK_NOPE_DIM
    rope_w = G * QK_ROPE_DIM
    for p in range(G // 2):
        x = r[:, nope_w + p * LANES:nope_w + (p + 1) * LANES]
        xs = r[:, nope_w + rope_w + p * LANES:nope_w + rope_w + (p + 1) * LANES]
        roped = (x * cos + xs * sin) * QK_EXP2_SCALE
        halves = (roped, pltpu.roll(roped, QK_ROPE_DIM, 1))
        for k in range(2):
            j = 2 * p + k
            nope = r[:, j * QK_NOPE_DIM:(j + 1) * QK_NOPE_DIM] * QK_EXP2_SCALE
            q_ref[0, j, :, :QK_NOPE_DIM] = nope.astype(BF16)
            q_ref[0, j, :, QK_NOPE_DIM:] = jnp.where(low, halves[k], 0.0).astype(BF16)


def _mla_q(cq, w_uq_r, cos, sin, tm=512):
    B, S, R = cq.shape
    G = HEAD_GROUP
    n_groups = MLA_HEADS // G
    wn = w_uq_r.shape[1] // n_groups
    return pl.pallas_call(
        _q_kernel,
        grid=(B, S // tm, n_groups),
        in_specs=[pl.BlockSpec((1, tm, R), lambda b, i, g: (b, i, 0)),
                  pl.BlockSpec((R, wn), lambda b, i, g: (0, g)),
                  pl.BlockSpec((1, tm, LANES), lambda b, i, g: (b, i, 0)),
                  pl.BlockSpec((1, tm, LANES), lambda b, i, g: (b, i, 0))],
        out_specs=pl.BlockSpec((1, G, tm, QK_PAD_DIM), lambda b, i, g: (b, g, i, 0)),
        out_shape=jax.ShapeDtypeStruct((B, MLA_HEADS, S, QK_PAD_DIM), BF16),
        compiler_params=_params(("arbitrary", "arbitrary", "arbitrary")),
        name="mla_q_up",
    )(cq, w_uq_r, cos, sin)


def _kv_kernel(ckv_ref, wk_ref, wvt_ref, kr_ref, k_ref, vt_ref):
    ckv = ckv_ref[0]
    kn = jnp.dot(ckv, wk_ref[...], preferred_element_type=F32)
    vt = lax.dot_general(wvt_ref[...], ckv, (((1,), (1,)), ((), ())),
                         preferred_element_type=F32)
    kr = kr_ref[0]
    for j in range(HEAD_GROUP):
        k_ref[0, j, :, :QK_NOPE_DIM] = kn[:, j * QK_NOPE_DIM:(j + 1) * QK_NOPE_DIM].astype(BF16)
        k_ref[0, j, :, QK_NOPE_DIM:] = kr
        vt_ref[0, j] = vt[j * V_HEAD_DIM:(j + 1) * V_HEAD_DIM, :].astype(BF16)


def _mla_kv(ckv, w_k, w_vt, kr, tm=512):
    B, S, R = ckv.shape
    G = HEAD_GROUP
    return pl.pallas_call(
        _kv_kernel,
        grid=(B, S // tm, MLA_HEADS // G),
        in_specs=[pl.BlockSpec((1, tm, R), lambda b, i, g: (b, i, 0)),
                  pl.BlockSpec((R, G * QK_NOPE_DIM), lambda b, i, g: (0, g)),
                  pl.BlockSpec((G * V_HEAD_DIM, R), lambda b, i, g: (g, 0)),
                  pl.BlockSpec((1, tm, LANES), lambda b, i, g: (b, i, 0))],
        out_specs=[pl.BlockSpec((1, G, tm, QK_PAD_DIM), lambda b, i, g: (b, g, i, 0)),
                   pl.BlockSpec((1, G, V_HEAD_DIM, tm), lambda b, i, g: (b, g, 0, i))],
        out_shape=[jax.ShapeDtypeStruct((B, MLA_HEADS, S, QK_PAD_DIM), BF16),
                   jax.ShapeDtypeStruct((B, MLA_HEADS, V_HEAD_DIM, S), BF16)],
        compiler_params=_params(("arbitrary", "arbitrary", "arbitrary")),
        name="mla_kv_up",
    )(ckv, w_k, w_vt, kr)


def _attn_kernel(q_ref, k_ref, vt_ref, o_ref, qt_ref, sa_ref, sb_ref, m_ref, l_ref, acc_ref,
                 *, tq):
    tk = tq // 2
    qi = pl.program_id(2)
    qt_ref[...] = q_ref[0, 0].T
    m_ref[...] = jnp.full(m_ref.shape, -jnp.inf, F32)
    l_ref[...] = jnp.zeros(l_ref.shape, F32)
    acc_ref[...] = jnp.zeros(acc_ref.shape, F32)

    def scores(j, s_ref, q0=0):
        start = pl.multiple_of(j * tk, tk)
        s_ref[:, q0:] = jnp.dot(k_ref[0, 0, pl.ds(start, tk), :], qt_ref[:, q0:],
                                preferred_element_type=F32)

    def update(j, s_ref, masked, q0=0):
        start = pl.multiple_of(j * tk, tk)
        s = s_ref[:, q0:]
        if masked:
            key = lax.broadcasted_iota(jnp.int32, s.shape, 0)
            qry = lax.broadcasted_iota(jnp.int32, s.shape, 1)
            s = jnp.where(key <= qry, s, -jnp.inf)
        m_prev = m_ref[:, q0:]
        m_new = jnp.maximum(m_prev, jnp.max(s, axis=0, keepdims=True))
        alpha = jnp.exp2(m_prev - m_new)
        p = jnp.exp2(s - m_new)
        l_ref[:, q0:] = alpha * l_ref[:, q0:] + jnp.sum(p, axis=0, keepdims=True)
        pv = jnp.dot(vt_ref[0, 0, :, pl.ds(start, tk)], p.astype(BF16), preferred_element_type=F32)
        acc_ref[:, q0:] = alpha * acc_ref[:, q0:] + pv
        m_ref[:, q0:] = m_new

    scores(0, sa_ref)

    def pair(i, carry):
        j = 2 * i
        scores(j + 1, sb_ref)
        update(j, sa_ref, False)
        scores(j + 2, sa_ref)
        update(j + 1, sb_ref, False)
        return carry

    lax.fori_loop(0, qi, pair, 0)
    scores(2 * qi + 1, sb_ref, q0=tk)
    update(2 * qi, sa_ref, True)
    update(2 * qi + 1, sb_ref, True, q0=tk)

    o_ref[0] = (acc_ref[...] / l_ref[...]).T.astype(o_ref.dtype)


def _mla_attention(q, k, vt, tq=1024):
    B, H, S, _ = q.shape
    return pl.pallas_call(
        functools.partial(_attn_kernel, tq=tq),
        grid=(B, H, S // tq),
        in_specs=[pl.BlockSpec((1, 1, tq, QK_PAD_DIM), lambda b, h, i: (b, h, i, 0)),
                  pl.BlockSpec((1, 1, S, QK_PAD_DIM), lambda b, h, i: (b, h, 0, 0)),
                  pl.BlockSpec((1, 1, V_HEAD_DIM, S), lambda b, h, i: (b, h, 0, 0))],
        out_specs=pl.BlockSpec((1, tq, V_HEAD_DIM), lambda b, h, i: (b, i, h)),
        out_shape=jax.ShapeDtypeStruct((B, S, H * V_HEAD_DIM), BF16),
        scratch_shapes=[pltpu.VMEM((QK_PAD_DIM, tq), BF16),
                        pltpu.VMEM((tq // 2, tq), F32), pltpu.VMEM((tq // 2, tq), F32),
                        pltpu.VMEM((1, tq), F32), pltpu.VMEM((1, tq), F32),
                        pltpu.VMEM((V_HEAD_DIM, tq), F32)],
        compiler_params=_params(("arbitrary", "arbitrary", "arbitrary")),
        name="mla_attention",
    )(q, k, vt)


def _proj_res_kernel(a_ref, w_ref, h_ref, g_ref, o_ref):
    r = jnp.dot(a_ref[0], w_ref[...], preferred_element_type=F32)
    o_ref[0] = h_ref[0] + g_ref[0] * r


def _proj_residual(a, w, h, gate, tm=512, tn=1024):
    B, S, K = a.shape
    N = w.shape[1]
    return pl.pallas_call(
        _proj_res_kernel,
        grid=(B, S // tm, N // tn),
        in_specs=[pl.BlockSpec((1, tm, K), lambda b, i, n: (b, i, 0)),
                  pl.BlockSpec((K, tn), lambda b, i, n: (0, n)),
                  pl.BlockSpec((1, tm, tn), lambda b, i, n: (b, i, n)),
                  pl.BlockSpec((1, 1, tn), lambda b, i, n: (b, 0, n))],
        out_specs=pl.BlockSpec((1, tm, tn), lambda b, i, n: (b, i, n)),
        out_shape=jax.ShapeDtypeStruct((B, S, N), F32),
        compiler_params=_params(("arbitrary", "arbitrary", "arbitrary")),
        name="proj_residual",
    )(a, w, h, gate)


def _mlp_kernel(h_ref, g_ref, sh_ref, sc_ref, gate_ref, w1_ref, w2_ref, fg_ref, o_ref, a_ref,
                *, final_norm):
    f = pl.program_id(2)
    tm, D = a_ref.shape

    @pl.when(f == 0)
    def _():
        for r in range(0, tm, ROW_CHUNK):
            rows = slice(r, r + ROW_CHUNK)
            a_ref[rows] = _rms_mod(h_ref[0, rows], g_ref[...], sh_ref[0], sc_ref[0]).astype(BF16)
        o_ref[0] = jnp.zeros((tm, D), F32)

    hid = jnp.dot(a_ref[...], w1_ref[...], preferred_element_type=F32)
    hid = jnp.square(jnp.maximum(hid, 0.0)).astype(BF16)
    for n in range(0, D, COL_CHUNK):
        cols = slice(n, n + COL_CHUNK)
        o_ref[0, :, cols] += jnp.dot(hid, w2_ref[:, cols], preferred_element_type=F32)

    @pl.when(f == pl.num_programs(2) - 1)
    def _():
        for r in range(0, tm, ROW_CHUNK):
            rows = slice(r, r + ROW_CHUNK)
            out = h_ref[0, rows] + gate_ref[0] * o_ref[0, rows]
            if final_norm:
                out = _rms(out, fg_ref[...])
            o_ref[0, rows] = out


def _mlp(h, g, sh, sc, gate, w1, w2, layer, final_g, final_norm, tm=512, tf=512):
    B, S, D = h.shape
    F = w1.shape[2]
    vec = pl.BlockSpec((1, D), lambda b, i, f: (0, 0))
    bvec = pl.BlockSpec((1, 1, D), lambda b, i, f: (b, 0, 0))
    return pl.pallas_call(
        functools.partial(_mlp_kernel, final_norm=final_norm),
        grid=(B, S // tm, F // tf),
        in_specs=[pl.BlockSpec((1, tm, D), lambda b, i, f: (b, i, 0), pipeline_mode=pl.Buffered(1)),
                  vec, bvec, bvec, bvec,
                  pl.BlockSpec((None, D, tf), lambda b, i, f: (layer, 0, f)),
                  pl.BlockSpec((None, tf, D), lambda b, i, f: (layer, f, 0)),
                  vec],
        out_specs=pl.BlockSpec((1, tm, D), lambda b, i, f: (b, i, 0)),
        out_shape=jax.ShapeDtypeStruct((B, S, D), F32),
        scratch_shapes=[pltpu.VMEM((tm, D), BF16)],
        compiler_params=_params(("arbitrary", "arbitrary", "arbitrary")),
        name="mlp",
    )(h, g, sh, sc, gate, w1, w2, final_g)


def _sg_in_kernel(h_ref, g_ref, sh_ref, sc_ref, w_ref, z_ref, mu_ref, rstd_ref, a_ref, s1_ref, s2_ref,
                  *, n_half, width):
    n = pl.program_id(2)

    @pl.when(n == 0)
    def _():
        for r in range(0, a_ref.shape[0], ROW_CHUNK):
            rows = slice(r, r + ROW_CHUNK)
            a_ref[rows] = _rms_mod(h_ref[0, rows], g_ref[...], sh_ref[0], sc_ref[0]).astype(BF16)
        s1_ref[...] = jnp.zeros(s1_ref.shape, F32)
        s2_ref[...] = jnp.zeros(s2_ref.shape, F32)

    half = (n >= n_half).astype(jnp.int32)
    for r in range(0, a_ref.shape[0], DOT_ROWS):
        rows = slice(r, r + DOT_ROWS)
        x = jnp.dot(a_ref[rows], w_ref[...], preferred_element_type=F32)
        z = 0.5 * x * (1.0 + lax.erf(x * math.sqrt(0.5)))
        z_ref[0, rows] = z
        zc = [z[:, j * LANES:(j + 1) * LANES] for j in range(z.shape[1] // LANES)]
        s1_ref[half, rows] += functools.reduce(lambda a, b: a + b, zc)
        s2_ref[half, rows] += functools.reduce(lambda a, b: a + b, [c * c for c in zc])

    @pl.when(n == pl.num_programs(2) - 1)
    def _():
        mu = jnp.sum(s1_ref[1], axis=-1, keepdims=True) / width
        var = jnp.sum(s2_ref[1], axis=-1, keepdims=True) / width - mu * mu
        mu_ref[0] = jnp.broadcast_to(mu, mu_ref.shape[1:])
        rstd_ref[0] = jnp.broadcast_to(lax.rsqrt(var + LN_EPS), rstd_ref.shape[1:])


def _sg_in(h, g, sh, sc, w_in, tm=1024, tn=1024):
    B, S, D = h.shape
    N = w_in.shape[1]
    width = N // 2
    vec = pl.BlockSpec((1, D), lambda b, i, n: (0, 0))
    bvec = pl.BlockSpec((1, 1, D), lambda b, i, n: (b, 0, 0))
    stat = pl.BlockSpec((1, tm, LANES), lambda b, i, n: (b, i, 0))
    return pl.pallas_call(
        functools.partial(_sg_in_kernel, n_half=width // tn, width=float(width)),
        grid=(B, S // tm, N // tn),
        in_specs=[pl.BlockSpec((1, tm, D), lambda b, i, n: (b, i, 0), pipeline_mode=pl.Buffered(1)),
                  vec, bvec, bvec,
                  pl.BlockSpec((D, tn), lambda b, i, n: (0, n))],
        out_specs=[pl.BlockSpec((1, tm, tn), lambda b, i, n: (b, i, n)), stat, stat],
        out_shape=[jax.ShapeDtypeStruct((B, S, N), F32),
                   jax.ShapeDtypeStruct((B, S, LANES), F32),
                   jax.ShapeDtypeStruct((B, S, LANES), F32)],
        scratch_shapes=[pltpu.VMEM((tm, D), BF16), pltpu.VMEM((2, tm, LANES), F32),
                        pltpu.VMEM((2, tm, LANES), F32)],
        compiler_params=_params(("arbitrary", "arbitrary", "arbitrary"), SG_IN_VMEM_LIMIT),
        name="sg_in",
    )(h, g, sh, sc, w_in)


def _sg_out_kernel(u_ref, v_ref, mu_ref, rstd_ref, lng_ref, lnb_ref, ws_ref, bs_ref, w_ref,
                   h_ref, gate_ref, o_ref, gated_ref, *, groups_per_step, group_dim):
    kk = pl.program_id(2)
    tm = u_ref.shape[1]
    T = SG_CHUNK

    @pl.when(kk == 0)
    def _():
        o_ref[0] = jnp.zeros(o_ref.shape[1:], F32)

    row = lax.broadcasted_iota(jnp.int32, (T, T), 0)
    col = lax.broadcasted_iota(jnp.int32, (T, T), 1)
    reps = group_dim // LANES
    ws = [jnp.where(col <= row, ws_ref[gg], 0.0).astype(BF16) for gg in range(groups_per_step)]
    for r in range(0, tm, DOT_ROWS):
        rows = slice(r, r + DOT_ROWS)
        for gg in range(groups_per_step):
            bias = jnp.concatenate([bs_ref[gg]] * reps, axis=1)
            c0, c1 = gg * group_dim, (gg + 1) * group_dim
            for r0 in range(r, r + DOT_ROWS, T):
                r1 = r0 + T
                mu = mu_ref[0, r0:r1, 0:1]
                rstd = rstd_ref[0, r0:r1, 0:1]
                vln = (v_ref[0, r0:r1, c0:c1] - mu) * rstd * lng_ref[:, c0:c1] + lnb_ref[:, c0:c1]
                mixed = jnp.dot(ws[gg], vln.astype(BF16), preferred_element_type=F32) + bias
                gated_ref[r0:r1, c0:c1] = (u_ref[0, r0:r1, c0:c1] * mixed).astype(BF16)
        for n in range(0, o_ref.shape[2], COL_CHUNK):
            cols = slice(n, n + COL_CHUNK)
            o_ref[0, rows, cols] += jnp.dot(gated_ref[rows], w_ref[:, cols], preferred_element_type=F32)

    @pl.when(kk == pl.num_programs(2) - 1)
    def _():
        for r in range(0, tm, ROW_CHUNK):
            rows = slice(r, r + ROW_CHUNK)
            o_ref[0, rows] = h_ref[0, rows] + gate_ref[0] * o_ref[0, rows]


def _sg_out(z, mu, rstd, ln_g, ln_b, w_s, bs_rep, w_out, h, gate, tm=512, groups_per_step=2):
    B, S, D = h.shape
    W = w_out.shape[0]
    group_dim = W // SG_GROUPS
    kc = groups_per_step * group_dim
    n_k = W // kc
    stat = pl.BlockSpec((1, tm, LANES), lambda b, i, k: (b, i, 0))
    return pl.pallas_call(
        functools.partial(_sg_out_kernel, groups_per_step=groups_per_step, group_dim=group_dim),
        grid=(B, S // tm, n_k),
        in_specs=[pl.BlockSpec((1, tm, kc), lambda b, i, k: (b, i, k)),
                  pl.BlockSpec((1, tm, kc), lambda b, i, k: (b, i, k + n_k)),
                  stat, stat,
                  pl.BlockSpec((1, kc), lambda b, i, k: (0, k)),
                  pl.BlockSpec((1, kc), lambda b, i, k: (0, k)),
                  pl.BlockSpec((groups_per_step, SG_CHUNK, SG_CHUNK), lambda b, i, k: (k, 0, 0)),
                  pl.BlockSpec((groups_per_step, SG_CHUNK, LANES), lambda b, i, k: (k, 0, 0)),
                  pl.BlockSpec((kc, D), lambda b, i, k: (k, 0)),
                  pl.BlockSpec((1, tm, D), lambda b, i, k: (b, i, 0), pipeline_mode=pl.Buffered(1)),
                  pl.BlockSpec((1, 1, D), lambda b, i, k: (b, 0, 0))],
        out_specs=pl.BlockSpec((1, tm, D), lambda b, i, k: (b, i, 0)),
        out_shape=jax.ShapeDtypeStruct((B, S, D), F32),
        scratch_shapes=[pltpu.VMEM((tm, kc), BF16)],
        compiler_params=_params(("arbitrary", "arbitrary", "arbitrary")),
        name="sg_out",
    )(z, z, mu, rstd, ln_g, ln_b, w_s, bs_rep, w_out, h, gate)


def _down_weight_layout(w_down):
    lat = Q_LORA_RANK + KV_LORA_RANK
    half = QK_ROPE_DIM // 2
    rope = w_down[:, lat:lat + QK_ROPE_DIM]
    swapped = jnp.concatenate([rope[:, half:], rope[:, :half]], axis=1)
    return jnp.concatenate([w_down[:, :lat], rope, rope, swapped, swapped], axis=1)


def _uq_weight_layout(w_uq):
    R = w_uq.shape[0]
    G = HEAD_GROUP
    half = QK_ROPE_DIM // 2
    w = w_uq.reshape(R, MLA_HEADS // G, G, QK_NOPE_DIM + QK_ROPE_DIM)
    nope = w[..., :QK_NOPE_DIM].reshape(R, MLA_HEADS // G, G * QK_NOPE_DIM)
    rope = w[..., QK_NOPE_DIM:]
    swapped = jnp.concatenate([rope[..., half:], rope[..., :half]], axis=-1)
    rope = rope.reshape(R, MLA_HEADS // G, G * QK_ROPE_DIM)
    swapped = swapped.reshape(R, MLA_HEADS // G, G * QK_ROPE_DIM)
    return jnp.concatenate([nope, rope, swapped], axis=-1).reshape(R, -1)


def _ukv_weight_layout(w_ukv):
    R = w_ukv.shape[0]
    w = w_ukv.reshape(R, MLA_HEADS, QK_NOPE_DIM + V_HEAD_DIM)
    w_k = w[..., :QK_NOPE_DIM].reshape(R, MLA_HEADS * QK_NOPE_DIM)
    w_vt = w[..., QK_NOPE_DIM:].reshape(R, MLA_HEADS * V_HEAD_DIM).T
    return w_k, w_vt


def kernel(x, c, positions, ada_w, ada_b, norm1_g, norm2_g, mla_w_down, mla_q_norm_g, mla_w_uq,
           mla_kv_norm_g, mla_w_ukv, mla_w_o, sg_w_in, sg_v_norm_g, sg_v_norm_b, sg_w_s, sg_b_s,
           sg_w_out, mlp_w1, mlp_w2, final_norm_g):
    B, S, D = x.shape
    depth = ada_w.shape[0]
    cos, sin = _rope_tables(positions)
    cos = cos.reshape(B, S, LANES)
    sin = sin.reshape(B, S, LANES)
    mod = _ada_mod(c, ada_w, ada_b).reshape(depth, B, N_MOD, 1, D)
    final_g = final_norm_g.reshape(1, D)
    w1_bf = mlp_w1.astype(BF16)
    w2_bf = mlp_w2.astype(BF16)

    h = x
    for i in range(depth):
        sh1, sc1, g1, sh2, sc2, g2 = [mod[i, :, m] for m in range(N_MOD)]
        n1 = norm1_g[i].reshape(1, D)
        j = i // 2
        if i % 2 == 0:
            w_down_r = _down_weight_layout(mla_w_down[j]).astype(BF16)
            w_uq_r = _uq_weight_layout(mla_w_uq[j]).astype(BF16)
            cq, ckv, kr = _mla_down(h, n1, sh1, sc1, w_down_r,
                                    mla_q_norm_g[j].reshape(1, -1), mla_kv_norm_g[j].reshape(1, -1),
                                    cos, sin)
            q = _mla_q(cq, w_uq_r, cos, sin)
            w_k, w_vt = _ukv_weight_layout(mla_w_ukv[j])
            k, vt = _mla_kv(ckv, w_k.astype(BF16), w_vt.astype(BF16), kr)
            o = _mla_attention(q, k, vt)
            h = _proj_residual(o, mla_w_o[j].astype(BF16), h, g1)
        else:
            z, mu, rstd = _sg_in(h, n1, sh1, sc1, sg_w_in[j].astype(BF16))
            bs_rep = jnp.broadcast_to(sg_b_s[j][:, :, None], (SG_GROUPS, SG_CHUNK, LANES))
            h = _sg_out(z, mu, rstd, sg_v_norm_g[j].reshape(1, -1), sg_v_norm_b[j].reshape(1, -1),
                        sg_w_s[j], bs_rep, sg_w_out[j].astype(BF16), h, g1)
        h = _mlp(h, norm2_g[i].reshape(1, D), sh2, sc2, g2, w1_bf, w2_bf, i, final_g,
                 final_norm=(i == depth - 1))
    return h
```

```python
import functools
import math

import jax
import jax.numpy as jnp
from jax import lax
from jax.experimental import pallas as pl
from jax.experimental.pallas import tpu as pltpu

F32 = jnp.float32
BF16 = jnp.bfloat16

MLA_HEADS = 32
QK_NOPE_DIM = 128
QK_ROPE_DIM = 64
V_HEAD_DIM = 128
Q_LORA_RANK = 1024
KV_LORA_RANK = 512
ROPE_THETA = 10000.0
SG_CHUNK = 128
SG_GROUPS = 32
NORM_EPS = 1e-6
LN_EPS = 1e-5
N_MOD = 6

LANES = 128
QK_PAD_DIM = 256
HEAD_GROUP = 8
QK_EXP2_SCALE = (QK_NOPE_DIM + QK_ROPE_DIM) ** -0.5 * math.log2(math.e)
VMEM_LIMIT = 56 * 1024 * 1024
SG_IN_VMEM_LIMIT = 60 * 1024 * 1024
MLP_VMEM_LIMIT = 62 * 1024 * 1024
ROW_CHUNK = 32
COL_CHUNK = 1024
DOT_ROWS = 256


def _params(semantics, vmem_limit=VMEM_LIMIT):
    return pltpu.CompilerParams(dimension_semantics=semantics, vmem_limit_bytes=vmem_limit)


def _rms_mod(h, g, sh, sc):
    ms = jnp.mean(h * h, axis=-1, keepdims=True)
    y = h * lax.rsqrt(ms + NORM_EPS) * g
    return y * (1.0 + sc) + sh


def _rms(x, g):
    ms = jnp.mean(x * x, axis=-1, keepdims=True)
    return x * lax.rsqrt(ms + NORM_EPS) * g


def _ada_kernel(c_ref, w_ref, b_ref, o_ref, *, batch, tn):
    w = w_ref[0]
    for b in range(batch):
        c = c_ref[b]
        ca = c / (1.0 + jnp.exp(-c))
        cols = [jnp.sum(w[:, j * LANES:(j + 1) * LANES] * ca, axis=0, keepdims=True)
                for j in range(tn // LANES)]
        o_ref[0, b:b + 1, :] = jnp.concatenate(cols, axis=1) + b_ref[0]


def _ada_mod(c, ada_w, ada_b, tn=512):
    L, D, N = ada_w.shape
    B = c.shape[0]
    c_rep = jnp.broadcast_to(c[:, :, None], (B, D, LANES))
    return pl.pallas_call(
        functools.partial(_ada_kernel, batch=B, tn=tn),
        grid=(L, N // tn),
        in_specs=[
            pl.BlockSpec((B, D, LANES), lambda l, n: (0, 0, 0)),
            pl.BlockSpec((1, D, tn), lambda l, n: (l, 0, n)),
            pl.BlockSpec((1, 1, tn), lambda l, n: (l, 0, n)),
        ],
        out_specs=pl.BlockSpec((1, B, tn), lambda l, n: (l, 0, n)),
        out_shape=jax.ShapeDtypeStruct((L, B, N), F32),
        compiler_params=_params(("arbitrary", "arbitrary")),
        name="ada_mod",
    )(c_rep, ada_w, ada_b.reshape(L, 1, N))


def _rope_kernel(pos_ref, invf_ref, sign_ref, cos_ref, sin_ref):
    ang = pos_ref[...].astype(F32) * invf_ref[...]
    cos_ref[...] = jnp.cos(ang)
    sin_ref[...] = jnp.sin(ang) * sign_ref[...]


def _rope_tables(positions, tm=1024):
    T = positions.size
    half = QK_ROPE_DIM // 2
    inv_freq = jnp.power(ROPE_THETA, -jnp.arange(0, QK_ROPE_DIM, 2, dtype=F32) / QK_ROPE_DIM)
    invf = jnp.tile(inv_freq, LANES // half).reshape(1, LANES)
    sign = jnp.tile(jnp.concatenate([-jnp.ones((half,), F32), jnp.ones((half,), F32)]),
                    LANES // QK_ROPE_DIM).reshape(1, LANES)
    row = pl.BlockSpec((tm, LANES), lambda i: (i, 0))
    const = pl.BlockSpec((1, LANES), lambda i: (0, 0))
    return pl.pallas_call(
        _rope_kernel,
        grid=(T // tm,),
        in_specs=[pl.BlockSpec((tm, 1), lambda i: (i, 0)), const, const],
        out_specs=[row, row],
        out_shape=[jax.ShapeDtypeStruct((T, LANES), F32)] * 2,
        compiler_params=_params(("arbitrary",)),
        name="rope_tables",
    )(positions.reshape(T, 1), invf, sign)


def _down_kernel(h_ref, g_ref, sh_ref, sc_ref, w_ref, qg_ref, kvg_ref, cos_ref, sin_ref,
                 cq_ref, ckv_ref, kr_ref):
    a = _rms_mod(h_ref[0], g_ref[...], sh_ref[0], sc_ref[0])
    down = jnp.dot(a.astype(BF16), w_ref[...], preferred_element_type=F32)
    q_end = Q_LORA_RANK
    kv_end = q_end + KV_LORA_RANK
    cq_ref[0] = _rms(down[:, :q_end], qg_ref[...]).astype(BF16)
    ckv_ref[0] = _rms(down[:, q_end:kv_end], kvg_ref[...]).astype(BF16)
    roped = (down[:, kv_end:kv_end + LANES] * cos_ref[0]
             + down[:, kv_end + LANES:kv_end + 2 * LANES] * sin_ref[0])
    lane = lax.broadcasted_iota(jnp.int32, roped.shape, 1)
    kr_ref[0] = jnp.where(lane < QK_ROPE_DIM, roped, 0.0).astype(BF16)


def _mla_down(h, g, sh, sc, w_down_r, q_g, kv_g, cos, sin, tm=256):
    B, S, D = h.shape
    N = w_down_r.shape[1]
    row = lambda n: pl.BlockSpec((1, tm, n), lambda b, i: (b, i, 0))
    vec = lambda n: pl.BlockSpec((1, n), lambda b, i: (0, 0))
    bvec = pl.BlockSpec((1, 1, D), lambda b, i: (b, 0, 0))
    return pl.pallas_call(
        _down_kernel,
        grid=(B, S // tm),
        in_specs=[row(D), vec(D), bvec, bvec,
                  pl.BlockSpec((D, N), lambda b, i: (0, 0), pipeline_mode=pl.Buffered(1)),
                  vec(Q_LORA_RANK), vec(KV_LORA_RANK), row(LANES), row(LANES)],
        out_specs=[row(Q_LORA_RANK), row(KV_LORA_RANK), row(LANES)],
        out_shape=[jax.ShapeDtypeStruct((B, S, Q_LORA_RANK), BF16),
                   jax.ShapeDtypeStruct((B, S, KV_LORA_RANK), BF16),
                   jax.ShapeDtypeStruct((B, S, LANES), BF16)],
        compiler_params=_params(("arbitrary", "arbitrary")),
        name="mla_down",
    )(h, g, sh, sc, w_down_r, q_g, kv_g, cos, sin)


def _q_kernel(cq_ref, w_ref, cos_ref, sin_ref, q_ref):
    G = HEAD_GROUP
    r = jnp.dot(cq_ref[0], w_ref[...], preferred_element_type=F32)
    cos = cos_ref[0]
    sin = sin_ref[0]
    lane = lax.broadcasted_iota(jnp.int32, cos.shape, 1)
    low = lane < QK_ROPE_DIM
    nope_w = G * QK_NOPE_DIM
    rope_w = G * QK_ROPE_DIM
    for p in range(G // 2):
        x = r[:, nope_w + p * LANES:nope_w + (p + 1) * LANES]
        xs = r[:, nope_w + rope_w + p * LANES:nope_w + rope_w + (p + 1) * LANES]
        roped = (x * cos + xs * sin) * QK_EXP2_SCALE
        halves = (roped, pltpu.roll(roped, QK_ROPE_DIM, 1))
        for k in range(2):
            j = 2 * p + k
            nope = r[:, j * QK_NOPE_DIM:(j + 1) * QK_NOPE_DIM] * QK_EXP2_SCALE
            q_ref[0, j, :, :QK_NOPE_DIM] = nope.astype(BF16)
            q_ref[0, j, :, QK_NOPE_DIM:] = jnp.where(low, halves[k], 0.0).astype(BF16)


def _mla_q(cq, w_uq_r, cos, sin, tm=512):
    B, S, R = cq.shape
    G = HEAD_GROUP
    n_groups = MLA_HEADS // G
    wn = w_uq_r.shape[1] // n_groups
    return pl.pallas_call(
        _q_kernel,
        grid=(B, S // tm, n_groups),
        in_specs=[pl.BlockSpec((1, tm, R), lambda b, i, g: (b, i, 0)),
                  pl.BlockSpec((R, wn), lambda b, i, g: (0, g)),
                  pl.BlockSpec((1, tm, LANES), lambda b, i, g: (b, i, 0)),
                  pl.BlockSpec((1, tm, LANES), lambda b, i, g: (b, i, 0))],
        out_specs=pl.BlockSpec((1, G, tm, QK_PAD_DIM), lambda b, i, g: (b, g, i, 0)),
        out_shape=jax.ShapeDtypeStruct((B, MLA_HEADS, S, QK_PAD_DIM), BF16),
        compiler_params=_params(("arbitrary", "arbitrary", "arbitrary")),
        name="mla_q_up",
    )(cq, w_uq_r, cos, sin)


def _kv_kernel(ckv_ref, wk_ref, wvt_ref, kr_ref, k_ref, vt_ref):
    ckv = ckv_ref[0]
    kn = jnp.dot(ckv, wk_ref[...], preferred_element_type=F32)
    vt = lax.dot_general(wvt_ref[...], ckv, (((1,), (1,)), ((), ())),
                         preferred_element_type=F32)
    kr = kr_ref[0]
    for j in range(HEAD_GROUP):
        k_ref[0, j, :, :QK_NOPE_DIM] = kn[:, j * QK_NOPE_DIM:(j + 1) * QK_NOPE_DIM].astype(BF16)
        k_ref[0, j, :, QK_NOPE_DIM:] = kr
        vt_ref[0, j] = vt[j * V_HEAD_DIM:(j + 1) * V_HEAD_DIM, :].astype(BF16)


def _mla_kv(ckv, w_k, w_vt, kr, tm=512):
    B, S, R = ckv.shape
    G = HEAD_GROUP
    return pl.pallas_call(
        _kv_kernel,
        grid=(B, S // tm, MLA_HEADS // G),
        in_specs=[pl.BlockSpec((1, tm, R), lambda b, i, g: (b, i, 0)),
                  pl.BlockSpec((R, G * QK_NOPE_DIM), lambda b, i, g: (0, g)),
                  pl.BlockSpec((G * V_HEAD_DIM, R), lambda b, i, g: (g, 0)),
                  pl.BlockSpec((1, tm, LANES), lambda b, i, g: (b, i, 0))],
        out_specs=[pl.BlockSpec((1, G, tm, QK_PAD_DIM), lambda b, i, g: (b, g, i, 0)),
                   pl.BlockSpec((1, G, V_HEAD_DIM, tm), lambda b, i, g: (b, g, 0, i))],
        out_shape=[jax.ShapeDtypeStruct((B, MLA_HEADS, S, QK_PAD_DIM), BF16),
                   jax.ShapeDtypeStruct((B, MLA_HEADS, V_HEAD_DIM, S), BF16)],
        compiler_params=_params(("arbitrary", "arbitrary", "arbitrary")),
        name="mla_kv_up",
    )(ckv, w_k, w_vt, kr)


def _attn_kernel(q_ref, k_ref, vt_ref, o_ref, qt_ref, sa_ref, sb_ref, m_ref, l_ref, acc_ref,
                 *, tq):
    tk = tq // 2
    qi = pl.program_id(2)
    qt_ref[...] = q_ref[0, 0].T
    m_ref[...] = jnp.full(m_ref.shape, -jnp.inf, F32)
    l_ref[...] = jnp.zeros(l_ref.shape, F32)
    acc_ref[...] = jnp.zeros(acc_ref.shape, F32)

    def scores(j, s_ref, q0=0):
        start = pl.multiple_of(j * tk, tk)
        s_ref[:, q0:] = jnp.dot(k_ref[0, 0, pl.ds(start, tk), :], qt_ref[:, q0:],
                                preferred_element_type=F32)

    def update(j, s_ref, masked, q0=0):
        start = pl.multiple_of(j * tk, tk)
        s = s_ref[:, q0:]
        if masked:
            key = lax.broadcasted_iota(jnp.int32, s.shape, 0)
            qry = lax.broadcasted_iota(jnp.int32, s.shape, 1)
            s = jnp.where(key <= qry, s, -jnp.inf)
        m_prev = m_ref[:, q0:]
        m_new = jnp.maximum(m_prev, jnp.max(s, axis=0, keepdims=True))
        alpha = jnp.exp2(m_prev - m_new)
        p = jnp.exp2(s - m_new)
        l_ref[:, q0:] = alpha * l_ref[:, q0:] + jnp.sum(p, axis=0, keepdims=True)
        pv = jnp.dot(vt_ref[0, 0, :, pl.ds(start, tk)], p.astype(BF16), preferred_element_type=F32)
        acc_ref[:, q0:] = alpha * acc_ref[:, q0:] + pv
        m_ref[:, q0:] = m_new

    scores(0, sa_ref)

    def pair(i, carry):
        j = 2 * i
        scores(j + 1, sb_ref)
        update(j, sa_ref, False)
        scores(j + 2, sa_ref)
        update(j + 1, sb_ref, False)
        return carry

    lax.fori_loop(0, qi, pair, 0)
    scores(2 * qi + 1, sb_ref, q0=tk)
    update(2 * qi, sa_ref, True)
    update(2 * qi + 1, sb_ref, True, q0=tk)

    o_ref[0] = (acc_ref[...] / l_ref[...]).T.astype(o_ref.dtype)


def _mla_attention(q, k, vt, tq=1024):
    B, H, S, _ = q.shape
    return pl.pallas_call(
        functools.partial(_attn_kernel, tq=tq),
        grid=(B, H, S // tq),
        in_specs=[pl.BlockSpec((1, 1, tq, QK_PAD_DIM), lambda b, h, i: (b, h, i, 0)),
                  pl.BlockSpec((1, 1, S, QK_PAD_DIM), lambda b, h, i: (b, h, 0, 0)),
                  pl.BlockSpec((1, 1, V_HEAD_DIM, S), lambda b, h, i: (b, h, 0, 0))],
        out_specs=pl.BlockSpec((1, tq, V_HEAD_DIM), lambda b, h, i: (b, i, h)),
        out_shape=jax.ShapeDtypeStruct((B, S, H * V_HEAD_DIM), BF16),
        scratch_shapes=[pltpu.VMEM((QK_PAD_DIM, tq), BF16),
                        pltpu.VMEM((tq // 2, tq), F32), pltpu.VMEM((tq // 2, tq), F32),
                        pltpu.VMEM((1, tq), F32), pltpu.VMEM((1, tq), F32),
                        pltpu.VMEM((V_HEAD_DIM, tq), F32)],
        compiler_params=_params(("arbitrary", "arbitrary", "arbitrary")),
        name="mla_attention",
    )(q, k, vt)


def _proj_res_kernel(a_ref, w_ref, h_ref, g_ref, o_ref):
    r = jnp.dot(a_ref[0], w_ref[...], preferred_element_type=F32)
    o_ref[0] = h_ref[0] + g_ref[0] * r


def _proj_residual(a, w, h, gate, tm=512, tn=1024):
    B, S, K = a.shape
    N = w.shape[1]
    return pl.pallas_call(
        _proj_res_kernel,
        grid=(B, S // tm, N // tn),
        in_specs=[pl.BlockSpec((1, tm, K), lambda b, i, n: (b, i, 0)),
                  pl.BlockSpec((K, tn), lambda b, i, n: (0, n)),
                  pl.BlockSpec((1, tm, tn), lambda b, i, n: (b, i, n)),
                  pl.BlockSpec((1, 1, tn), lambda b, i, n: (b, 0, n))],
        out_specs=pl.BlockSpec((1, tm, tn), lambda b, i, n: (b, i, n)),
        out_shape=jax.ShapeDtypeStruct((B, S, N), F32),
        compiler_params=_params(("arbitrary", "arbitrary", "arbitrary")),
        name="proj_residual",
    )(a, w, h, gate)


def _mlp_kernel(h_ref, g_ref, sh_ref, sc_ref, gate_ref, w1_ref, w2_ref, fg_ref, o_ref, a_ref,
                *, final_norm):
    f = pl.program_id(2)
    tm, D = a_ref.shape

    @pl.when(f == 0)
    def _():
        for r in range(0, tm, ROW_CHUNK):
            rows = slice(r, r + ROW_CHUNK)
            a_ref[rows] = _rms_mod(h_ref[0, rows], g_ref[...], sh_ref[0], sc_ref[0]).astype(BF16)
        o_ref[0] = jnp.zeros((tm, D), F32)

    hid = jnp.dot(a_ref[...], w1_ref[...], preferred_element_type=F32)
    hid = jnp.square(jnp.maximum(hid, 0.0)).astype(BF16)
    for n in range(0, D, COL_CHUNK):
        cols = slice(n, n + COL_CHUNK)
        o_ref[0, :, cols] += jnp.dot(hid, w2_ref[:, cols], preferred_element_type=F32)

    @pl.when(f == pl.num_programs(2) - 1)
    def _():
        for r in range(0, tm, ROW_CHUNK):
            rows = slice(r, r + ROW_CHUNK)
            out = h_ref[0, rows] + gate_ref[0] * o_ref[0, rows]
            if final_norm:
                out = _rms(out, fg_ref[...])
            o_ref[0, rows] = out


def _mlp(h, g, sh, sc, gate, w1, w2, layer, final_g, final_norm, tm=512, tf=512):
    B, S, D = h.shape
    F = w1.shape[2]
    vec = pl.BlockSpec((1, D), lambda b, i, f: (0, 0))
    bvec = pl.BlockSpec((1, 1, D), lambda b, i, f: (b, 0, 0))
    return pl.pallas_call(
        functools.partial(_mlp_kernel, final_norm=final_norm),
        grid=(B, S // tm, F // tf),
        in_specs=[pl.BlockSpec((1, tm, D), lambda b, i, f: (b, i, 0)),
                  vec, bvec, bvec, bvec,
                  pl.BlockSpec((None, D, tf), lambda b, i, f: (layer, 0, f)),
                  pl.BlockSpec((None, tf, D), lambda b, i, f: (layer, f, 0)),
                  vec],
        out_specs=pl.BlockSpec((1, tm, D), lambda b, i, f: (b, i, 0)),
        out_shape=jax.ShapeDtypeStruct((B, S, D), F32),
        scratch_shapes=[pltpu.VMEM((tm, D), BF16)],
        compiler_params=_params(("arbitrary", "arbitrary", "arbitrary"), MLP_VMEM_LIMIT),
        name="mlp",
    )(h, g, sh, sc, gate, w1, w2, final_g)


def _sg_in_kernel(h_ref, g_ref, sh_ref, sc_ref, w_ref, z_ref, mu_ref, rstd_ref, a_ref, s1_ref, s2_ref,
                  *, n_half, width):
    n = pl.program_id(2)

    @pl.when(n == 0)
    def _():
        for r in range(0, a_ref.shape[0], ROW_CHUNK):
            rows = slice(r, r + ROW_CHUNK)
            a_ref[rows] = _rms_mod(h_ref[0, rows], g_ref[...], sh_ref[0], sc_ref[0]).astype(BF16)
        s1_ref[...] = jnp.zeros(s1_ref.shape, F32)
        s2_ref[...] = jnp.zeros(s2_ref.shape, F32)

    half = (n >= n_half).astype(jnp.int32)
    for r in range(0, a_ref.shape[0], DOT_ROWS):
        rows = slice(r, r + DOT_ROWS)
        x = jnp.dot(a_ref[rows], w_ref[...], preferred_element_type=F32)
        z = 0.5 * x * (1.0 + lax.erf(x * math.sqrt(0.5)))
        z_ref[0, rows] = z
        zc = [z[:, j * LANES:(j + 1) * LANES] for j in range(z.shape[1] // LANES)]
        s1_ref[half, rows] += functools.reduce(lambda a, b: a + b, zc)
        s2_ref[half, rows] += functools.reduce(lambda a, b: a + b, [c * c for c in zc])

    @pl.when(n == pl.num_programs(2) - 1)
    def _():
        mu = jnp.sum(s1_ref[1], axis=-1, keepdims=True) / width
        var = jnp.sum(s2_ref[1], axis=-1, keepdims=True) / width - mu * mu
        mu_ref[0] = jnp.broadcast_to(mu, mu_ref.shape[1:])
        rstd_ref[0] = jnp.broadcast_to(lax.rsqrt(var + LN_EPS), rstd_ref.shape[1:])


def _sg_in(h, g, sh, sc, w_in, tm=1024, tn=1024):
    B, S, D = h.shape
    N = w_in.shape[1]
    width = N // 2
    vec = pl.BlockSpec((1, D), lambda b, i, n: (0, 0))
    bvec = pl.BlockSpec((1, 1, D), lambda b, i, n: (b, 0, 0))
    stat = pl.BlockSpec((1, tm, LANES), lambda b, i, n: (b, i, 0))
    return pl.pallas_call(
        functools.partial(_sg_in_kernel, n_half=width // tn, width=float(width)),
        grid=(B, S // tm, N // tn),
        in_specs=[pl.BlockSpec((1, tm, D), lambda b, i, n: (b, i, 0), pipeline_mode=pl.Buffered(1)),
                  vec, bvec, bvec,
                  pl.BlockSpec((D, tn), lambda b, i, n: (0, n))],
        out_specs=[pl.BlockSpec((1, tm, tn), lambda b, i, n: (b, i, n)), stat, stat],
        out_shape=[jax.ShapeDtypeStruct((B, S, N), F32),
                   jax.ShapeDtypeStruct((B, S, LANES), F32),
                   jax.ShapeDtypeStruct((B, S, LANES), F32)],
        scratch_shapes=[pltpu.VMEM((tm, D), BF16), pltpu.VMEM((2, tm, LANES), F32),
                        pltpu.VMEM((2, tm, LANES), F32)],
        compiler_params=_params(("arbitrary", "arbitrary", "arbitrary"), SG_IN_VMEM_LIMIT),
        name="sg_in",
    )(h, g, sh, sc, w_in)


def _sg_out_kernel(u_ref, v_ref, mu_ref, rstd_ref, lng_ref, lnb_ref, ws_ref, bs_ref, w_ref,
                   h_ref, gate_ref, o_ref, gated_ref, *, groups_per_step, group_dim):
    kk = pl.program_id(2)
    tm = u_ref.shape[1]
    T = SG_CHUNK

    @pl.when(kk == 0)
    def _():
        o_ref[0] = jnp.zeros(o_ref.shape[1:], F32)

    row = lax.broadcasted_iota(jnp.int32, (T, T), 0)
    col = lax.broadcasted_iota(jnp.int32, (T, T), 1)
    reps = group_dim // LANES
    ws = [jnp.where(col <= row, ws_ref[gg], 0.0).astype(BF16) for gg in range(groups_per_step)]
    for r in range(0, tm, DOT_ROWS):
        rows = slice(r, r + DOT_ROWS)
        for gg in range(groups_per_step):
            bias = jnp.concatenate([bs_ref[gg]] * reps, axis=1)
            c0, c1 = gg * group_dim, (gg + 1) * group_dim
            for r0 in range(r, r + DOT_ROWS, T):
                r1 = r0 + T
                mu = mu_ref[0, r0:r1, 0:1]
                rstd = rstd_ref[0, r0:r1, 0:1]
                vln = (v_ref[0, r0:r1, c0:c1] - mu) * rstd * lng_ref[:, c0:c1] + lnb_ref[:, c0:c1]
                mixed = jnp.dot(ws[gg], vln.astype(BF16), preferred_element_type=F32) + bias
                gated_ref[r0:r1, c0:c1] = (u_ref[0, r0:r1, c0:c1] * mixed).astype(BF16)
        for n in range(0, o_ref.shape[2], COL_CHUNK):
            cols = slice(n, n + COL_CHUNK)
            o_ref[0, rows, cols] += jnp.dot(gated_ref[rows], w_ref[:, cols], preferred_element_type=F32)

    @pl.when(kk == pl.num_programs(2) - 1)
    def _():
        for r in range(0, tm, ROW_CHUNK):
            rows = slice(r, r + ROW_CHUNK)
            o_ref[0, rows] = h_ref[0, rows] + gate_ref[0] * o_ref[0, rows]


def _sg_out(z, mu, rstd, ln_g, ln_b, w_s, bs_rep, w_out, h, gate, tm=512, groups_per_step=2):
    B, S, D = h.shape
    W = w_out.shape[0]
    group_dim = W // SG_GROUPS
    kc = groups_per_step * group_dim
    n_k = W // kc
    stat = pl.BlockSpec((1, tm, LANES), lambda b, i, k: (b, i, 0))
    return pl.pallas_call(
        functools.partial(_sg_out_kernel, groups_per_step=groups_per_step, group_dim=group_dim),
        grid=(B, S // tm, n_k),
        in_specs=[pl.BlockSpec((1, tm, kc), lambda b, i, k: (b, i, k)),
                  pl.BlockSpec((1, tm, kc), lambda b, i, k: (b, i, k + n_k)),
                  stat, stat,
                  pl.BlockSpec((1, kc), lambda b, i, k: (0, k)),
                  pl.BlockSpec((1, kc), lambda b, i, k: (0, k)),
                  pl.BlockSpec((groups_per_step, SG_CHUNK, SG_CHUNK), lambda b, i, k: (k, 0, 0)),
                  pl.BlockSpec((groups_per_step, SG_CHUNK, LANES), lambda b, i, k: (k, 0, 0)),
                  pl.BlockSpec((kc, D), lambda b, i, k: (k, 0)),
                  pl.BlockSpec((1, tm, D), lambda b, i, k: (b, i, 0), pipeline_mode=pl.Buffered(1)),
                  pl.BlockSpec((1, 1, D), lambda b, i, k: (b, 0, 0))],
        out_specs=pl.BlockSpec((1, tm, D), lambda b, i, k: (b, i, 0)),
        out_shape=jax.ShapeDtypeStruct((B, S, D), F32),
        scratch_shapes=[pltpu.VMEM((tm, kc), BF16)],
        compiler_params=_params(("arbitrary", "arbitrary", "arbitrary")),
        name="sg_out",
    )(z, z, mu, rstd, ln_g, ln_b, w_s, bs_rep, w_out, h, gate)


def _down_weight_layout(w_down):
    lat = Q_LORA_RANK + KV_LORA_RANK
    half = QK_ROPE_DIM // 2
    rope = w_down[:, lat:lat + QK_ROPE_DIM]
    swapped = jnp.concatenate([rope[:, half:], rope[:, :half]], axis=1)
    return jnp.concatenate([w_down[:, :lat], rope, rope, swapped, swapped], axis=1)


def _uq_weight_layout(w_uq):
    R = w_uq.shape[0]
    G = HEAD_GROUP
    half = QK_ROPE_DIM // 2
    w = w_uq.reshape(R, MLA_HEADS // G, G, QK_NOPE_DIM + QK_ROPE_DIM)
    nope = w[..., :QK_NOPE_DIM].reshape(R, MLA_HEADS // G, G * QK_NOPE_DIM)
    rope = w[..., QK_NOPE_DIM:]
    swapped = jnp.concatenate([rope[..., half:], rope[..., :half]], axis=-1)
    rope = rope.reshape(R, MLA_HEADS // G, G * QK_ROPE_DIM)
    swapped = swapped.reshape(R, MLA_HEADS // G, G * QK_ROPE_DIM)
    return jnp.concatenate([nope, rope, swapped], axis=-1).reshape(R, -1)


def _ukv_weight_layout(w_ukv):
    R = w_ukv.shape[0]
    w = w_ukv.reshape(R, MLA_HEADS, QK_NOPE_DIM + V_HEAD_DIM)
    w_k = w[..., :QK_NOPE_DIM].reshape(R, MLA_HEADS * QK_NOPE_DIM)
    w_vt = w[..., QK_NOPE_DIM:].reshape(R, MLA_HEADS * V_HEAD_DIM).T
    return w_k, w_vt


def kernel(x, c, positions, ada_w, ada_b, norm1_g, norm2_g, mla_w_down, mla_q_norm_g, mla_w_uq,
           mla_kv_norm_g, mla_w_ukv, mla_w_o, sg_w_in, sg_v_norm_g, sg_v_norm_b, sg_w_s, sg_b_s,
           sg_w_out, mlp_w1, mlp_w2, final_norm_g):
    B, S, D = x.shape
    depth = ada_w.shape[0]
    cos, sin = _rope_tables(positions)
    cos = cos.reshape(B, S, LANES)
    sin = sin.reshape(B, S, LANES)
    mod = _ada_mod(c, ada_w, ada_b).reshape(depth, B, N_MOD, 1, D)
    final_g = final_norm_g.reshape(1, D)
    w1_bf = mlp_w1.astype(BF16)
    w2_bf = mlp_w2.astype(BF16)

    h = x
    for i in range(depth):
        sh1, sc1, g1, sh2, sc2, g2 = [mod[i, :, m] for m in range(N_MOD)]
        n1 = norm1_g[i].reshape(1, D)
        j = i // 2
        if i % 2 == 0:
            w_down_r = _down_weight_layout(mla_w_down[j]).astype(BF16)
            w_uq_r = _uq_weight_layout(mla_w_uq[j]).astype(BF16)
            cq, ckv, kr = _mla_down(h, n1, sh1, sc1, w_down_r,
                                    mla_q_norm_g[j].reshape(1, -1), mla_kv_norm_g[j].reshape(1, -1),
                                    cos, sin)
            q = _mla_q(cq, w_uq_r, cos, sin)
            w_k, w_vt = _ukv_weight_layout(mla_w_ukv[j])
            k, vt = _mla_kv(ckv, w_k.astype(BF16), w_vt.astype(BF16), kr)
            o = _mla_attention(q, k, vt)
            h = _proj_residual(o, mla_w_o[j].astype(BF16), h, g1)
        else:
            z, mu, rstd = _sg_in(h, n1, sh1, sc1, sg_w_in[j].astype(BF16))
            bs_rep = jnp.broadcast_to(sg_b_s[j][:, :, None], (SG_GROUPS, SG_CHUNK, LANES))
            h = _sg_out(z, mu, rstd, sg_v_norm_g[j].reshape(1, -1), sg_v_norm_b[j].reshape(1, -1),
                        sg_w_s[j], bs_rep, sg_w_out[j].astype(BF16), h, g1)
        h = _mlp(h, norm2_g[i].reshape(1, D), sh2, sc2, g2, w1_bf, w2_bf, i, final_g,
                 final_norm=(i == depth - 1))
    return h
```
